```python
import jax, jax.numpy as jnp
from jax import lax
import numpy as np

D_MODEL = 2048
BATCH = 4
SEQ = 4096
DEPTH = 1

PLE_DIM = 256
D_MIX = D_MODEL
GLA_WIDTH = D_MIX // 2
CONV_WIDTH = D_MIX - GLA_WIDTH
GLA_HEADS = 4
GLA_DV = GLA_WIDTH // GLA_HEADS
GLA_DK = GLA_DV // 2
GLA_KEY_WIDTH = GLA_HEADS * GLA_DK
GATE_RANK = 16
GATE_TAU = 16.0
CHUNK = 64
CONV_K = 31
CONV_GROUPS = 8
CONV_GROUP_WIDTH = CONV_WIDTH // CONV_GROUPS
EPS = 1e-6

Q_END = GLA_KEY_WIDTH
K_END = Q_END + GLA_KEY_WIDTH
V_END = K_END + GLA_WIDTH
ALOW_END = V_END + GATE_RANK
GA_END = ALOW_END + GLA_WIDTH
CVAL_END = GA_END + CONV_WIDTH
CGATE_END = CVAL_END + CONV_WIDTH
D_IN = CGATE_END + CONV_WIDTH

kernel_name = "hybrid_gla_conformer_ple"


def rmsnorm(x, g):
    xf = x.astype(jnp.float32)
    y = xf * lax.rsqrt(jnp.mean(xf * xf, axis=-1, keepdims=True) + EPS)
    return (y * g.astype(jnp.float32)).astype(x.dtype)


def gla_chunked(q, k, v, log_a):
    B, H, S, DK = q.shape
    DV = v.shape[-1]
    N = S // CHUNK

    def to_chunks(t):
        return jnp.moveaxis(t.reshape(B, H, N, CHUNK, t.shape[-1]), 2, 0)

    qc, kc, vc, gc = to_chunks(q), to_chunks(k), to_chunks(v), to_chunks(log_a)
    bc = jnp.cumsum(gc, axis=-2)
    causal = jnp.tril(jnp.ones((CHUNK, CHUNK), dtype=bool))[:, :, None]

    def step(state, inp):
        qi, ki, vi, bi = inp
        diff = bi[..., :, None, :] - bi[..., None, :, :]
        decay = jnp.exp(jnp.where(causal, diff, -jnp.inf))
        scores = jnp.einsum('bhid,bhjd,bhijd->bhij', qi, ki, decay)
        o = (jnp.einsum('bhij,bhjv->bhiv', scores, vi)
             + jnp.einsum('bhid,bhdv->bhiv', qi * jnp.exp(bi), state))
        b_last = bi[..., -1:, :]
        state = (state * jnp.exp(b_last)[..., 0, :, None]
                 + jnp.einsum('bhjd,bhjv->bhdv', ki * jnp.exp(b_last - bi), vi))
        return state, o

    state0 = jnp.zeros((B, H, DK, DV), jnp.float32)
    _, oc = lax.scan(step, state0, (qc, kc, vc, bc))
    return jnp.moveaxis(oc, 0, 2).reshape(B, H, S, DV)


def hybrid_layer(h, p_i, norm_mix, w_in, w_alpha, b_alpha, gla_norm, conv_w, conv_b,
                 conv_ln_g, conv_ln_b, w_out, ple_norm, w_ple_gate, b_ple_gate, w_ple):
    B, S, _ = h.shape
    u = rmsnorm(h, norm_mix)
    proj = u @ w_in
    q, k, v, a_low, g_a, c_val, c_gate, g_b = jnp.split(
        proj, [Q_END, K_END, V_END, ALOW_END, GA_END, CVAL_END, CGATE_END], axis=-1)

    def heads(t, d):
        return t.reshape(B, S, GLA_HEADS, d).transpose(0, 2, 1, 3).astype(jnp.float32)

    log_a = jax.nn.log_sigmoid((a_low @ w_alpha + b_alpha).astype(jnp.float32)) / GATE_TAU
    o = gla_chunked(heads(q, GLA_DK) * (GLA_DK ** -0.5), heads(k, GLA_DK),
                    heads(v, GLA_DV), heads(log_a, GLA_DK))
    o = rmsnorm(o.transpose(0, 2, 1, 3), gla_norm)
    y_a = o.reshape(B, S, GLA_WIDTH).astype(h.dtype) * jax.nn.silu(g_a)

    c = c_val * jax.nn.sigmoid(c_gate)
    c = lax.conv_general_dilated(
        c, conv_w[:, None, :], window_strides=(1,), padding=[(CONV_K - 1, 0)],
        dimension_numbers=('NWC', 'WIO', 'NWC'),
        feature_group_count=CONV_WIDTH) + conv_b
    cg = c.reshape(B, S, CONV_GROUPS, CONV_GROUP_WIDTH).astype(jnp.float32)
    mu = jnp.mean(cg, axis=-1, keepdims=True)
    var = jnp.mean(jnp.square(cg - mu), axis=-1, keepdims=True)
    cg = ((cg - mu) * lax.rsqrt(var + EPS)).reshape(B, S, CONV_WIDTH)
    c = cg * conv_ln_g + conv_ln_b
    y_b = jax.nn.silu(c).astype(h.dtype) * jax.nn.silu(g_b)

    h = h + jnp.concatenate([y_a, y_b], axis=-1) @ w_out

    gate = jax.nn.sigmoid(rmsnorm(h, ple_norm) @ w_ple_gate + b_ple_gate)
    h = h + gate * (p_i @ w_ple)
    return h


def setup_inputs(seed: int = 0) -> dict:
    key = jax.random.key(seed)
    ks = jax.random.split(key, 18)
    f32 = jnp.float32
    nrm = lambda k, shape, s: jax.random.normal(k, shape, f32) * s
    L = DEPTH
    return {
        "x": nrm(ks[0], (BATCH, SEQ, D_MODEL), 1.0),
        "p": nrm(ks[1], (DEPTH, BATCH, SEQ, PLE_DIM), 1.0),
        "norm_mix": 1.0 + nrm(ks[2], (L, D_MODEL), 0.02),
        "w_in": nrm(ks[3], (L, D_MODEL, D_IN), D_MODEL ** -0.5),
        "w_alpha": nrm(ks[4], (L, GATE_RANK, GLA_KEY_WIDTH), GATE_RANK ** -0.5),
        "b_alpha": nrm(ks[5], (L, GLA_KEY_WIDTH), 0.5),
        "gla_norm": 1.0 + nrm(ks[6], (L, GLA_DV), 0.02),
        "conv_w": nrm(ks[7], (L, CONV_K, CONV_WIDTH), CONV_K ** -0.5),
        "conv_b": nrm(ks[8], (L, CONV_WIDTH), 0.02),
        "conv_ln_g": 1.0 + nrm(ks[9], (L, CONV_WIDTH), 0.02),
        "conv_ln_b": nrm(ks[10], (L, CONV_WIDTH), 0.02),
        "w_out": nrm(ks[11], (L, D_MIX, D_MODEL), D_MIX ** -0.5),
        "ple_norm": 1.0 + nrm(ks[12], (L, D_MODEL), 0.02),
        "w_ple_gate": nrm(ks[13], (L, D_MODEL, D_MODEL), D_MODEL ** -0.5),
        "b_ple_gate": nrm(ks[14], (L, D_MODEL), 0.02),
        "w_ple": nrm(ks[15], (L, PLE_DIM, D_MODEL), PLE_DIM ** -0.5),
        "final_norm": 1.0 + nrm(ks[16], (D_MODEL,), 0.02),
    }


def reference(x, p, norm_mix, w_in, w_alpha, b_alpha, gla_norm, conv_w, conv_b,
              conv_ln_g, conv_ln_b, w_out, ple_norm, w_ple_gate, b_ple_gate, w_ple,
              final_norm):
    h = x
    for i in range(DEPTH):
        h = hybrid_layer(h, p[i], norm_mix[i], w_in[i], w_alpha[i], b_alpha[i],
                         gla_norm[i], conv_w[i], conv_b[i], conv_ln_g[i], conv_ln_b[i],
                         w_out[i], ple_norm[i], w_ple_gate[i], b_ple_gate[i], w_ple[i])
    return rmsnorm(h, final_norm)
```

```python
import functools

import numpy as np
import jax
import jax.numpy as jnp
from jax import lax
from jax.experimental import pallas as pl
from jax.experimental.pallas import tpu as pltpu

F32 = jnp.float32
BF16 = jnp.bfloat16

D_MODEL = 2048
PLE_DIM = 256
GLA_WIDTH = 1024
CONV_WIDTH = 1024
GLA_HEADS = 4
GLA_DV = GLA_WIDTH // GLA_HEADS
GLA_DK = GLA_DV // 2
GLA_KEY_WIDTH = GLA_HEADS * GLA_DK
GATE_RANK = 16
GATE_TAU = 16.0
CONV_K = 31
CONV_GROUPS = 8
CONV_GROUP_WIDTH = CONV_WIDTH // CONV_GROUPS
EPS = 1e-6

SUBLANES = 8
LANES = 128

Q_END = GLA_KEY_WIDTH
K_END = Q_END + GLA_KEY_WIDTH
V_END = K_END + GLA_WIDTH
ALOW_END = V_END + GATE_RANK
GA_END = ALOW_END + GLA_WIDTH
CVAL_END = GA_END + CONV_WIDTH
CGATE_END = CVAL_END + CONV_WIDTH
D_IN = CGATE_END + CONV_WIDTH

O_Q = 0
O_K = O_Q + GLA_KEY_WIDTH
O_V = O_K + GLA_KEY_WIDTH
O_GA = O_V + GLA_WIDTH
O_CVAL = O_GA + GLA_WIDTH
O_CGATE = O_CVAL + CONV_WIDTH
O_GB = O_CGATE + CONV_WIDTH
O_ALOW = O_GB + CONV_WIDTH
D_IN_PAD = O_ALOW + LANES

SEQ_TILE = 256
TOKEN_TILE = 256
CONV_HIST = 32
DIAG_LEVEL = 0
VMEM_LIMIT = 56 * 1024 * 1024


def _sigmoid(x):
    return 1.0 / (1.0 + jnp.exp(-x))


def _dot(a, b):
    return jnp.dot(a, b, preferred_element_type=F32)


def _dot_nt(a, b):
    return lax.dot_general(a, b, (((1,), (1,)), ((), ())), preferred_element_type=F32)


def _dot_tn(a, b):
    return lax.dot_general(a, b, (((0,), (0,)), ((), ())), preferred_element_type=F32)


def _level_sizes(ts):
    sizes = []
    s = ts // 2
    while s >= SUBLANES:
        sizes.append(s)
        s //= 2
    return tuple(sizes)


def _level_map(ts):
    i = np.arange(ts)[:, None]
    j = np.arange(ts)[None, :]
    x = np.bitwise_xor(i, j)
    top = np.floor(np.log2(np.maximum(x, 1))).astype(np.int32)
    lvl = np.where(j > i, -1, np.where(i // SUBLANES == j // SUBLANES, DIAG_LEVEL, top))
    return lvl.astype(np.int32)


def _mixer_kernel(x_ref, nm_ref, win_ref, wal_ref, bal_ref, gn_ref, cw_ref, cb_ref, lg_ref, lb_ref,
                  tril_ref, lvl_ref, mix_ref, state_ref, cbuf_ref):
    ts = SEQ_TILE

    @pl.when(pl.program_id(1) == 0)
    def _():
        state_ref[...] = jnp.zeros_like(state_ref)
        cbuf_ref[0:CONV_HIST, :] = jnp.zeros((CONV_HIST, CONV_WIDTH), F32)

    x = x_ref[0]
    ms = jnp.mean(x * x, axis=-1, keepdims=True)
    u = (x * lax.rsqrt(ms + EPS) * nm_ref[...]).astype(BF16)

    def proj(lo, width):
        return _dot(u, win_ref[:, lo:lo + width])

    a_low = proj(O_ALOW, LANES).astype(BF16)
    z = _dot(a_low, wal_ref[...]) + bal_ref[...]
    g = (jnp.minimum(z, 0.0) - jnp.log1p(jnp.exp(-jnp.abs(z)))) * (1.0 / GATE_TAU)
    g_hi = g.astype(BF16)
    r1 = g - g_hi.astype(F32)
    g_mid = r1.astype(BF16)
    g_lo = (r1 - g_mid.astype(F32)).astype(BF16)
    tril = tril_ref[...]
    b_all = _dot(tril, g_hi) + _dot(tril, g_mid) + _dot(tril, g_lo)

    q_all = proj(O_Q, GLA_KEY_WIDTH) * (GLA_DK ** -0.5)
    k_all = proj(O_K, GLA_KEY_WIDTH)
    v_all = proj(O_V, GLA_WIDTH).astype(BF16)
    ga_all = proj(O_GA, GLA_WIDTH)
    lvl = lvl_ref[...]
    nblk = ts // SUBLANES
    lane = lax.broadcasted_iota(jnp.int32, (nblk, SUBLANES, LANES), 2)

    for h in range(GLA_HEADS):
        ksl = slice(h * GLA_DK, (h + 1) * GLA_DK)
        vsl = slice(h * GLA_DV, (h + 1) * GLA_DV)
        q = q_all[:, ksl]
        k = k_all[:, ksl]
        b = b_all[:, ksl]
        v = v_all[:, vsl]

        p = jnp.zeros((ts, ts), F32)
        for s in _level_sizes(ts):
            b3 = b.reshape(ts // (2 * s), 2 * s, GLA_DK)
            a = jnp.exp(-jnp.abs(b3 - b3[:, s - 1:s, :])).reshape(ts, GLA_DK)
            p_l = _dot_nt((q * a).astype(BF16), (k * a).astype(BF16))
            p = jnp.where(lvl == int(np.log2(s)), p_l, p)

        b3 = b.reshape(nblk, SUBLANES, GLA_DK)
        q3 = q.reshape(nblk, SUBLANES, GLA_DK)
        k3 = k.reshape(nblk, SUBLANES, GLA_DK)
        compact = jnp.zeros((nblk, SUBLANES, LANES), F32)
        for d in range(SUBLANES):
            t = q3 * jnp.exp(jnp.minimum(b3 - b3[:, d:d + 1, :], 0.0)) * k3[:, d:d + 1, :]
            compact = jnp.where((lane & (SUBLANES - 1)) == d, jnp.sum(t, axis=-1, keepdims=True), compact)
        compact = compact.reshape(ts, LANES)
        p = jnp.where(lvl == DIAG_LEVEL, jnp.concatenate([compact] * (ts // LANES), axis=1), p)

        st = state_ref[h]
        b_last = b[ts - 1:ts, :]
        o = _dot(p.astype(BF16), v) + _dot_nt((q * jnp.exp(b)).astype(BF16), st.astype(BF16))
        k_hat = (k * jnp.exp(b_last - b)).astype(BF16)
        state_ref[h] = st * jnp.exp(b_last) + _dot_tn(v, k_hat)

        o_ms = jnp.mean(o * o, axis=-1, keepdims=True)
        o_n = o * lax.rsqrt(o_ms + EPS) * gn_ref[...]
        ga = ga_all[:, vsl]
        mix_ref[0, :, vsl] = (o_n * (ga * _sigmoid(ga))).astype(BF16)

    c = proj(O_CVAL, CONV_WIDTH) * _sigmoid(proj(O_CGATE, CONV_WIDTH))
    cbuf_ref[CONV_HIST:CONV_HIST + ts, :] = c
    acc = jnp.broadcast_to(cb_ref[...], (ts, CONV_WIDTH))
    for tap in range(CONV_K):
        start = CONV_HIST - (CONV_K - 1) + tap
        acc = acc + cw_ref[tap:tap + 1, :] * cbuf_ref[start:start + ts, :]
    cbuf_ref[0:CONV_HIST, :] = cbuf_ref[ts:ts + CONV_HIST, :]

    gb_all = proj(O_GB, CONV_WIDTH)
    for gi in range(CONV_GROUPS):
        gsl = slice(gi * CONV_GROUP_WIDTH, (gi + 1) * CONV_GROUP_WIDTH)
        cg = acc[:, gsl]
        mu = jnp.mean(cg, axis=-1, keepdims=True)
        dlt = cg - mu
        var = jnp.mean(dlt * dlt, axis=-1, keepdims=True)
        y = dlt * lax.rsqrt(var + EPS) * lg_ref[:, gsl] + lb_ref[:, gsl]
        gb = gb_all[:, gsl]
        mix_ref[0, :, GLA_WIDTH + gi * CONV_GROUP_WIDTH:GLA_WIDTH + (gi + 1) * CONV_GROUP_WIDTH] = (
            (y * _sigmoid(y)) * (gb * _sigmoid(gb))).astype(BF16)


def _output_kernel(h_ref, mix_ref, p_ref, wout_ref, pn_ref, wg_ref, bg_ref, wp_ref, fn_ref, o_ref, *,
                   apply_final_norm):
    h = h_ref[...] + _dot(mix_ref[...], wout_ref[...])
    ms = jnp.mean(h * h, axis=-1, keepdims=True)
    n = (h * lax.rsqrt(ms + EPS) * pn_ref[...]).astype(BF16)
    gate = _sigmoid(_dot(n, wg_ref[...]) + bg_ref[...])
    h = h + gate * _dot(p_ref[...].astype(BF16), wp_ref[...])
    if apply_final_norm:
        ms2 = jnp.mean(h * h, axis=-1, keepdims=True)
        h = h * lax.rsqrt(ms2 + EPS) * fn_ref[...]
    o_ref[...] = h


def _resident(shape):
    nd = len(shape)
    return pl.BlockSpec(shape, lambda *_: (0,) * nd, pipeline_mode=pl.Buffered(1))


def _mixer_call(h, norm_mix, w_in_r, w_alpha_p, b_alpha, gla_norm, conv_w, conv_b, ln_g, ln_b, tril, lvl):
    bsz, seq, _ = h.shape
    ts = SEQ_TILE
    row = lambda a: a.reshape(1, -1)
    args = (h, row(norm_mix), w_in_r, w_alpha_p, row(b_alpha), row(gla_norm), conv_w, row(conv_b),
            row(ln_g), row(ln_b), tril, lvl)
    in_specs = [pl.BlockSpec((1, ts, D_MODEL), lambda b, s: (b, s, 0))]
    in_specs += [_resident(a.shape) for a in args[1:]]
    return pl.pallas_call(
        _mixer_kernel,
        out_shape=jax.ShapeDtypeStruct((bsz, seq, D_MODEL), BF16),
        grid=(bsz, seq // ts),
        in_specs=in_specs,
        out_specs=pl.BlockSpec((1, ts, D_MODEL), lambda b, s: (b, s, 0)),
        scratch_shapes=[
            pltpu.VMEM((GLA_HEADS, GLA_DV, GLA_DK), F32),
            pltpu.VMEM((CONV_HIST + ts, CONV_WIDTH), F32),
        ],
        compiler_params=pltpu.CompilerParams(
            dimension_semantics=("arbitrary", "arbitrary"), vmem_limit_bytes=VMEM_LIMIT),
        name="gla_conv_mixer",
    )(*args)


def _output_call(h2d, mix2d, p2d, w_out, ple_norm, w_gate, b_gate, w_ple, final_norm, apply_final_norm):
    tokens = h2d.shape[0]
    tm = TOKEN_TILE
    row = lambda a: a.reshape(1, -1)
    tile = lambda width: pl.BlockSpec((tm, width), lambda i: (i, 0))
    weights = (w_out, row(ple_norm), w_gate, row(b_gate), w_ple, row(final_norm))
    return pl.pallas_call(
        functools.partial(_output_kernel, apply_final_norm=apply_final_norm),
        out_shape=jax.ShapeDtypeStruct((tokens, D_MODEL), F32),
        grid=(tokens // tm,),
        in_specs=[tile(D_MODEL), tile(D_MODEL), tile(PLE_DIM)] + [_resident(a.shape) for a in weights],
        out_specs=tile(D_MODEL),
        compiler_params=pltpu.CompilerParams(
            dimension_semantics=("arbitrary",), vmem_limit_bytes=VMEM_LIMIT),
        name="outproj_ple",
    )(h2d, mix2d, p2d, *weights)


def kernel(x, p, norm_mix, w_in, w_alpha, b_alpha, gla_norm, conv_w, conv_b, conv_ln_g, conv_ln_b,
           w_out, ple_norm, w_ple_gate, b_ple_gate, w_ple, final_norm):
    bsz, seq, _ = x.shape
    depth = p.shape[0]
    assert seq % SEQ_TILE == 0 and (bsz * seq) % TOKEN_TILE == 0
    tril = jnp.asarray(np.tril(np.ones((SEQ_TILE, SEQ_TILE), np.float32)), BF16)
    lvl = jnp.asarray(_level_map(SEQ_TILE))

    h = x
    for i in range(depth):
        wi = w_in[i]
        w_in_r = jnp.concatenate(
            [wi[:, :V_END], wi[:, ALOW_END:], wi[:, V_END:ALOW_END],
             jnp.zeros((D_MODEL, LANES - GATE_RANK), wi.dtype)], axis=1).astype(BF16)
        w_alpha_p = jnp.concatenate(
            [w_alpha[i], jnp.zeros((LANES - GATE_RANK, GLA_KEY_WIDTH), w_alpha.dtype)], axis=0).astype(BF16)
        conv_w_p = jnp.concatenate([conv_w[i], jnp.zeros((1, CONV_WIDTH), conv_w.dtype)], axis=0)
        mix = _mixer_call(h, norm_mix[i], w_in_r, w_alpha_p, b_alpha[i], gla_norm[i], conv_w_p,
                          conv_b[i], conv_ln_g[i], conv_ln_b[i], tril, lvl)
        h = _output_call(
            h.reshape(bsz * seq, D_MODEL), mix.reshape(bsz * seq, D_MODEL), p[i].reshape(bsz * seq, PLE_DIM),
            w_out[i].astype(BF16), ple_norm[i], w_ple_gate[i].astype(BF16), b_ple_gate[i],
            w_ple[i].astype(BF16), final_norm, apply_final_norm=(i == depth - 1),
        ).reshape(bsz, seq, D_MODEL)
    return h
```

```python
import functools

import numpy as np
import jax
import jax.numpy as jnp
from jax import lax
from jax.experimental import pallas as pl
from jax.experimental.pallas import tpu as pltpu

F32 = jnp.float32
BF16 = jnp.bfloat16

D_MODEL = 2048
PLE_DIM = 256
GLA_WIDTH = 1024
CONV_WIDTH = 1024
GLA_HEADS = 4
GLA_DV = GLA_WIDTH // GLA_HEADS
GLA_DK = GLA_DV // 2
GLA_KEY_WIDTH = GLA_HEADS * GLA_DK
GATE_RANK = 16
GATE_TAU = 16.0
CONV_K = 31
CONV_GROUPS = 8
CONV_GROUP_WIDTH = CONV_WIDTH // CONV_GROUPS
EPS = 1e-6

SUBLANES = 8
LANES = 128

Q_END = GLA_KEY_WIDTH
K_END = Q_END + GLA_KEY_WIDTH
V_END = K_END + GLA_WIDTH
ALOW_END = V_END + GATE_RANK
GA_END = ALOW_END + GLA_WIDTH
CVAL_END = GA_END + CONV_WIDTH
CGATE_END = CVAL_END + CONV_WIDTH
D_IN = CGATE_END + CONV_WIDTH

O_Q = 0
O_K = O_Q + GLA_KEY_WIDTH
O_V = O_K + GLA_KEY_WIDTH
O_GA = O_V + GLA_WIDTH
O_CVAL = O_GA + GLA_WIDTH
O_CGATE = O_CVAL + CONV_WIDTH
O_GB = O_CGATE + CONV_WIDTH
O_ALOW = O_GB + CONV_WIDTH
D_IN_PAD = O_ALOW + LANES

SEQ_TILE = 256
TOKEN_TILE = 256
CONV_HIST = 32
DIAG_LEVEL = 0
VMEM_LIMIT = 56 * 1024 * 1024


def _sigmoid(x):
    return 1.0 / (1.0 + jnp.exp(-x))


def _dot(a, b):
    return jnp.dot(a, b, preferred_element_type=F32)


def _dot_nt(a, b):
    return lax.dot_general(a, b, (((1,), (1,)), ((), ())), preferred_element_type=F32)


def _dot_tn(a, b):
    return lax.dot_general(a, b, (((0,), (0,)), ((), ())), preferred_element_type=F32)


def _level_sizes(ts):
    sizes = []
    s = ts // 2
    while s >= SUBLANES:
        sizes.append(s)
        s //= 2
    return tuple(sizes)


def _level_map(ts):
    i = np.arange(ts)[:, None]
    j = np.arange(ts)[None, :]
    x = np.bitwise_xor(i, j)
    top = np.floor(np.log2(np.maximum(x, 1))).astype(np.int32)
    lvl = np.where(j > i, -1, np.where(i // SUBLANES == j // SUBLANES, DIAG_LEVEL, top))
    return lvl.astype(np.int32)


def _mixer_kernel(x_ref, nm_ref, win_ref, wal_ref, bal_ref, gn_ref, cw_ref, cb_ref, lg_ref, lb_ref,
                  tril_ref, lvl_ref, mix_ref, state_ref, cbuf_ref):
    ts = SEQ_TILE

    @pl.when(pl.program_id(1) == 0)
    def _():
        state_ref[...] = jnp.zeros_like(state_ref)
        cbuf_ref[0:CONV_HIST, :] = jnp.zeros((CONV_HIST, CONV_WIDTH), F32)

    x = x_ref[0]
    ms = jnp.mean(x * x, axis=-1, keepdims=True)
    u = (x * lax.rsqrt(ms + EPS) * nm_ref[...]).astype(BF16)

    def proj(lo, width):
        return _dot(u, win_ref[:, lo:lo + width])

    a_low = proj(O_ALOW, LANES).astype(BF16)
    z = _dot(a_low, wal_ref[...]) + bal_ref[...]
    g = (jnp.minimum(z, 0.0) - jnp.log1p(jnp.exp(-jnp.abs(z)))) * (1.0 / GATE_TAU)
    g_hi = g.astype(BF16)
    r1 = g - g_hi.astype(F32)
    g_mid = r1.astype(BF16)
    g_lo = (r1 - g_mid.astype(F32)).astype(BF16)
    tril = tril_ref[...]
    b_all = _dot(tril, g_hi) + _dot(tril, g_mid) + _dot(tril, g_lo)

    q_all = proj(O_Q, GLA_KEY_WIDTH) * (GLA_DK ** -0.5)
    k_all = proj(O_K, GLA_KEY_WIDTH)
    v_all = proj(O_V, GLA_WIDTH).astype(BF16)
    ga_all = proj(O_GA, GLA_WIDTH)
    lvl = lvl_ref[...]
    nblk = ts // SUBLANES
    lane = lax.broadcasted_iota(jnp.int32, (nblk, SUBLANES, LANES), 2)

    for h in range(GLA_HEADS):
        ksl = slice(h * GLA_DK, (h + 1) * GLA_DK)
        vsl = slice(h * GLA_DV, (h + 1) * GLA_DV)
        q = q_all[:, ksl]
        k = k_all[:, ksl]
        b = b_all[:, ksl]
        v = v_all[:, vsl]

        p = jnp.zeros((ts, ts), F32)
        for s in _level_sizes(ts):
            b3 = b.reshape(ts // (2 * s), 2 * s, GLA_DK)
            a = jnp.exp(-jnp.abs(b3 - b3[:, s - 1:s, :])).reshape(ts, GLA_DK)
            p_l = _dot_nt((q * a).astype(BF16), (k * a).astype(BF16))
            p = jnp.where(lvl == int(np.log2(s)), p_l, p)

        b3 = b.reshape(nblk, SUBLANES, GLA_DK)
        q3 = q.reshape(nblk, SUBLANES, GLA_DK)
        k3 = k.reshape(nblk, SUBLANES, GLA_DK)
        compact = jnp.zeros((nblk, SUBLANES, LANES), F32)
        for d in range(SUBLANES):
            t = q3 * jnp.exp(jnp.minimum(b3 - b3[:, d:d + 1, :], 0.0)) * k3[:, d:d + 1, :]
            compact = jnp.where((lane & (SUBLANES - 1)) == d, jnp.sum(t, axis=-1, keepdims=True), compact)
        compact = compact.reshape(ts, LANES)
        p = jnp.where(lvl == DIAG_LEVEL, jnp.concatenate([compact] * (ts // LANES), axis=1), p)

        st = state_ref[h]
        b_last = b[ts - 1:ts, :]
        o = _dot(p.astype(BF16), v) + _dot_nt((q * jnp.exp(b)).astype(BF16), st.astype(BF16))
        k_hat = (k * jnp.exp(b_last - b)).astype(BF16)
        state_ref[h] = st * jnp.exp(b_last) + _dot_tn(v, k_hat)

        o_ms = jnp.mean(o * o, axis=-1, keepdims=True)
        o_n = o * lax.rsqrt(o_ms + EPS) * gn_ref[...]
        ga = ga_all[:, vsl]
        mix_ref[0, :, vsl] = (o_n * (ga * _sigmoid(ga))).astype(BF16)

    c = proj(O_CVAL, CONV_WIDTH) * _sigmoid(proj(O_CGATE, CONV_WIDTH))
    cbuf_ref[CONV_HIST:CONV_HIST + ts, :] = c
    acc = jnp.broadcast_to(cb_ref[...], (ts, CONV_WIDTH))
    for r in range(SUBLANES):
        v_r = None
        for a in range((CONV_K - 1 - r) // SUBLANES + 1):
            tap = CONV_K - 1 - (SUBLANES * a + r)
            lo = CONV_HIST - SUBLANES * (a + 1)
            term = cw_ref[tap:tap + 1, :] * cbuf_ref[lo:lo + ts + SUBLANES, :]
            v_r = term if v_r is None else v_r + term
        acc = acc + v_r[SUBLANES - r:SUBLANES - r + ts]
    cbuf_ref[0:CONV_HIST, :] = cbuf_ref[ts:ts + CONV_HIST, :]

    gb_all = proj(O_GB, CONV_WIDTH)
    for gi in range(CONV_GROUPS):
        gsl = slice(gi * CONV_GROUP_WIDTH, (gi + 1) * CONV_GROUP_WIDTH)
        cg = acc[:, gsl]
        mu = jnp.mean(cg, axis=-1, keepdims=True)
        dlt = cg - mu
        var = jnp.mean(dlt * dlt, axis=-1, keepdims=True)
        y = dlt * lax.rsqrt(var + EPS) * lg_ref[:, gsl] + lb_ref[:, gsl]
        gb = gb_all[:, gsl]
        mix_ref[0, :, GLA_WIDTH + gi * CONV_GROUP_WIDTH:GLA_WIDTH + (gi + 1) * CONV_GROUP_WIDTH] = (
            (y * _sigmoid(y)) * (gb * _sigmoid(gb))).astype(BF16)


def _output_kernel(h_ref, mix_ref, p_ref, wout_ref, pn_ref, wg_ref, bg_ref, wp_ref, fn_ref, o_ref, *,
                   apply_final_norm):
    h = h_ref[...] + _dot(mix_ref[...], wout_ref[...])
    ms = jnp.mean(h * h, axis=-1, keepdims=True)
    n = (h * lax.rsqrt(ms + EPS) * pn_ref[...]).astype(BF16)
    gate = _sigmoid(_dot(n, wg_ref[...]) + bg_ref[...])
    h = h + gate * _dot(p_ref[...].astype(BF16), wp_ref[...])
    if apply_final_norm:
        ms2 = jnp.mean(h * h, axis=-1, keepdims=True)
        h = h * lax.rsqrt(ms2 + EPS) * fn_ref[...]
    o_ref[...] = h


def _resident(shape):
    nd = len(shape)
    return pl.BlockSpec(shape, lambda *_: (0,) * nd, pipeline_mode=pl.Buffered(1))


def _mixer_call(h, norm_mix, w_in_r, w_alpha_p, b_alpha, gla_norm, conv_w, conv_b, ln_g, ln_b, tril, lvl):
    bsz, seq, _ = h.shape
    ts = SEQ_TILE
    row = lambda a: a.reshape(1, -1)
    args = (h, row(norm_mix), w_in_r, w_alpha_p, row(b_alpha), row(gla_norm), conv_w, row(conv_b),
            row(ln_g), row(ln_b), tril, lvl)
    in_specs = [pl.BlockSpec((1, ts, D_MODEL), lambda b, s: (b, s, 0))]
    in_specs += [_resident(a.shape) for a in args[1:]]
    return pl.pallas_call(
        _mixer_kernel,
        out_shape=jax.ShapeDtypeStruct((bsz, seq, D_MODEL), BF16),
        grid=(bsz, seq // ts),
        in_specs=in_specs,
        out_specs=pl.BlockSpec((1, ts, D_MODEL), lambda b, s: (b, s, 0)),
        scratch_shapes=[
            pltpu.VMEM((GLA_HEADS, GLA_DV, GLA_DK), F32),
            pltpu.VMEM((CONV_HIST + ts, CONV_WIDTH), F32),
        ],
        compiler_params=pltpu.CompilerParams(
            dimension_semantics=("arbitrary", "arbitrary"), vmem_limit_bytes=VMEM_LIMIT),
        name="gla_conv_mixer",
    )(*args)


def _output_call(h2d, mix2d, p2d, w_out, ple_norm, w_gate, b_gate, w_ple, final_norm, apply_final_norm):
    tokens = h2d.shape[0]
    tm = TOKEN_TILE
    row = lambda a: a.reshape(1, -1)
    tile = lambda width: pl.BlockSpec((tm, width), lambda i: (i, 0))
    weights = (w_out, row(ple_norm), w_gate, row(b_gate), w_ple, row(final_norm))
    return pl.pallas_call(
        functools.partial(_output_kernel, apply_final_norm=apply_final_norm),
        out_shape=jax.ShapeDtypeStruct((tokens, D_MODEL), F32),
        grid=(tokens // tm,),
        in_specs=[tile(D_MODEL), tile(D_MODEL), tile(PLE_DIM)] + [_resident(a.shape) for a in weights],
        out_specs=tile(D_MODEL),
        compiler_params=pltpu.CompilerParams(
            dimension_semantics=("arbitrary",), vmem_limit_bytes=VMEM_LIMIT),
        name="outproj_ple",
    )(h2d, mix2d, p2d, *weights)


def kernel(x, p, norm_mix, w_in, w_alpha, b_alpha, gla_norm, conv_w, conv_b, conv_ln_g, conv_ln_b,
           w_out, ple_norm, w_ple_gate, b_ple_gate, w_ple, final_norm):
    bsz, seq, _ = x.shape
    depth = p.shape[0]
    assert seq % SEQ_TILE == 0 and (bsz * seq) % TOKEN_TILE == 0
    tril = jnp.asarray(np.tril(np.ones((SEQ_TILE, SEQ_TILE), np.float32)), BF16)
    lvl = jnp.asarray(_level_map(SEQ_TILE))

    h = x
    for i in range(depth):
        wi = w_in[i].astype(BF16)
        w_in_r = jnp.concatenate(
            [wi[:, :V_END], wi[:, ALOW_END:], wi[:, V_END:ALOW_END],
             jnp.zeros((D_MODEL, LANES - GATE_RANK), BF16)], axis=1)
        w_alpha_p = jnp.concatenate(
            [w_alpha[i], jnp.zeros((LANES - GATE_RANK, GLA_KEY_WIDTH), w_alpha.dtype)], axis=0).astype(BF16)
        conv_w_p = jnp.concatenate([conv_w[i], jnp.zeros((1, CONV_WIDTH), conv_w.dtype)], axis=0)
        mix = _mixer_call(h, norm_mix[i], w_in_r, w_alpha_p, b_alpha[i], gla_norm[i], conv_w_p,
                          conv_b[i], conv_ln_g[i], conv_ln_b[i], tril, lvl)
        h = _output_call(
            h.reshape(bsz * seq, D_MODEL), mix.reshape(bsz * seq, D_MODEL), p[i].reshape(bsz * seq, PLE_DIM),
            w_out[i].astype(BF16), ple_norm[i], w_ple_gate[i].astype(BF16), b_ple_gate[i],
            w_ple[i].astype(BF16), final_norm, apply_final_norm=(i == depth - 1),
        ).reshape(bsz, seq, D_MODEL)
    return h
```

```python
import functools
import math

import numpy as np
import jax
import jax.numpy as jnp
from jax import lax
from jax.experimental import pallas as pl
from jax.experimental.pallas import tpu as pltpu

F32 = jnp.float32
BF16 = jnp.bfloat16

D_MODEL = 2048
PLE_DIM = 256
GLA_WIDTH = 1024
CONV_WIDTH = 1024
GLA_HEADS = 4
GLA_DV = GLA_WIDTH // GLA_HEADS
GLA_DK = GLA_DV // 2
GLA_KEY_WIDTH = GLA_HEADS * GLA_DK
GATE_RANK = 16
GATE_TAU = 16.0
CONV_K = 31
CONV_GROUPS = 8
CONV_GROUP_WIDTH = CONV_WIDTH // CONV_GROUPS
EPS = 1e-6
LOG2E = math.log2(math.e)

SUBLANES = 8
LANES = 128

Q_END = GLA_KEY_WIDTH
K_END = Q_END + GLA_KEY_WIDTH
V_END = K_END + GLA_WIDTH
ALOW_END = V_END + GATE_RANK
GA_END = ALOW_END + GLA_WIDTH
CVAL_END = GA_END + CONV_WIDTH
CGATE_END = CVAL_END + CONV_WIDTH
D_IN = CGATE_END + CONV_WIDTH

O_Q = 0
O_K = O_Q + GLA_KEY_WIDTH
O_V = O_K + GLA_KEY_WIDTH
O_CVAL = O_V + GLA_WIDTH
O_CGATE = O_CVAL + CONV_WIDTH
O_GA = O_CGATE + CONV_WIDTH
O_GB = O_GA + GLA_WIDTH
O_ALOW = O_GB + CONV_WIDTH
D_IN_PAD = O_ALOW + LANES

SEQ_TILE = 256
TOKEN_TILE = 512
PREP_ROWS = 256
CONV_HIST = 32
DIAG_LEVEL = 0
VMEM_LIMIT = 56 * 1024 * 1024


def _sigmoid(x):
    return 1.0 / (1.0 + jnp.exp(-x))


def _dot(a, b):
    return jnp.dot(a, b, preferred_element_type=F32)


def _dot_nt(a, b):
    return lax.dot_general(a, b, (((1,), (1,)), ((), ())), preferred_element_type=F32)


def _dot_tn(a, b):
    return lax.dot_general(a, b, (((0,), (0,)), ((), ())), preferred_element_type=F32)


def _level_sizes(ts):
    sizes = []
    s = ts // 2
    while s >= SUBLANES:
        sizes.append(s)
        s //= 2
    return tuple(sizes)


def _level_map(ts):
    i = np.arange(ts)[:, None]
    j = np.arange(ts)[None, :]
    x = np.bitwise_xor(i, j)
    top = np.floor(np.log2(np.maximum(x, 1))).astype(np.int32)
    lvl = np.where(j > i, -1, np.where(i // SUBLANES == j // SUBLANES, DIAG_LEVEL, top))
    return lvl.astype(np.int32)


def _mixer_kernel(x_ref, nm_ref, win_ref, wal_ref, bal_ref, gn_ref, cw_ref, cb_ref, lg_ref, lb_ref,
                  tril_ref, lvl_ref, mix_ref, state_ref, cbuf_ref, *, tiles_per_seq):
    ts = SEQ_TILE
    nblk = ts // SUBLANES

    @pl.when(lax.rem(pl.program_id(0), tiles_per_seq) == 0)
    def _():
        state_ref[...] = jnp.zeros_like(state_ref)
        cbuf_ref[0:CONV_HIST, :] = jnp.zeros((CONV_HIST, CONV_WIDTH), F32)

    x = x_ref[...]
    ms = jnp.mean(x * x, axis=-1, keepdims=True)
    u = (x * lax.rsqrt(ms + EPS) * nm_ref[...]).astype(BF16)

    def proj(lo, width):
        return _dot(u, win_ref[:, lo:lo + width])

    a_low = proj(O_ALOW, LANES).astype(BF16)
    z = _dot(a_low, wal_ref[...]) + bal_ref[...]
    g = (jnp.minimum(z, 0.0) - jnp.log1p(jnp.exp(-jnp.abs(z)))) * (1.0 / GATE_TAU)
    g_hi = g.astype(BF16)
    r1 = g - g_hi.astype(F32)
    g_mid = r1.astype(BF16)
    g_lo = (r1 - g_mid.astype(F32)).astype(BF16)
    tril3 = jnp.concatenate([tril_ref[...]] * 3, axis=1)
    b_all = _dot(tril3, jnp.concatenate([g_hi, g_mid, g_lo], axis=0)) * LOG2E

    qkv = proj(O_Q, O_CVAL - O_Q)
    q_all = qkv[:, O_Q:O_K] * (GLA_DK ** -0.5)
    k_all = qkv[:, O_K:O_V]
    v_all = qkv[:, O_V:O_CVAL].astype(BF16)
    ga_all = proj(O_GA, GLA_WIDTH)
    lvl = lvl_ref[...]
    lane_mod = lax.broadcasted_iota(jnp.int32, (nblk, SUBLANES, LANES), 2) & (SUBLANES - 1)

    for h in range(GLA_HEADS):
        ksl = slice(h * GLA_DK, (h + 1) * GLA_DK)
        vsl = slice(h * GLA_DV, (h + 1) * GLA_DV)
        q = q_all[:, ksl]
        k = k_all[:, ksl]
        b = b_all[:, ksl]
        v = v_all[:, vsl]

        p = jnp.zeros((ts, ts), F32)
        for s in _level_sizes(ts):
            b3 = b.reshape(ts // (2 * s), 2 * s, GLA_DK)
            a = jnp.exp2(-jnp.abs(b3 - b3[:, s - 1:s, :])).reshape(ts, GLA_DK)
            p_l = _dot_nt((q * a).astype(BF16), (k * a).astype(BF16))
            p = jnp.where(lvl == int(np.log2(s)), p_l, p)

        b3 = b.reshape(nblk, SUBLANES, GLA_DK)
        q3 = q.reshape(nblk, SUBLANES, GLA_DK)
        k3 = k.reshape(nblk, SUBLANES, GLA_DK)
        compact = jnp.zeros((nblk, SUBLANES, LANES), F32)
        for d in range(SUBLANES):
            t = q3 * jnp.exp2(b3 - b3[:, d:d + 1, :]) * k3[:, d:d + 1, :]
            compact = jnp.where(lane_mod == d, jnp.sum(t, axis=-1, keepdims=True), compact)
        compact = compact.reshape(ts, LANES)
        p = jnp.where(lvl == DIAG_LEVEL, jnp.concatenate([compact] * (ts // LANES), axis=1), p)

        st = state_ref[h]
        b_last = b[ts - 1:ts, :]
        o = _dot(p.astype(BF16), v) + _dot_nt((q * jnp.exp2(b)).astype(BF16), st.astype(BF16))
        k_hat = (k * jnp.exp2(b_last - b)).astype(BF16)
        state_ref[h] = st * jnp.exp2(b_last) + _dot_tn(v, k_hat)

        o_ms = jnp.mean(o * o, axis=-1, keepdims=True)
        o_n = o * lax.rsqrt(o_ms + EPS) * gn_ref[...]
        ga = ga_all[:, vsl]
        mix_ref[:, vsl] = (o_n * (ga * _sigmoid(ga))).astype(BF16)

    glu = proj(O_CVAL, O_GA - O_CVAL)
    cbuf_ref[CONV_HIST:CONV_HIST + ts, :] = glu[:, :CONV_WIDTH] * _sigmoid(glu[:, CONV_WIDTH:])
    acc = jnp.broadcast_to(cb_ref[...], (ts, CONV_WIDTH))
    for r in range(SUBLANES):
        v_r = None
        for a in range((CONV_K - 1 - r) // SUBLANES + 1):
            tap = CONV_K - 1 - (SUBLANES * a + r)
            lo = CONV_HIST - SUBLANES * (a + 1)
            term = cw_ref[tap:tap + 1, :] * cbuf_ref[lo:lo + ts + SUBLANES, :]
            v_r = term if v_r is None else v_r + term
        shifted = v_r if r == 0 else pltpu.roll(v_r, r, axis=0)
        acc = acc + shifted[SUBLANES:SUBLANES + ts]
    cbuf_ref[0:CONV_HIST, :] = cbuf_ref[ts:ts + CONV_HIST, :]

    gb_all = proj(O_GB, CONV_WIDTH)
    for gi in range(CONV_GROUPS):
        gsl = slice(gi * CONV_GROUP_WIDTH, (gi + 1) * CONV_GROUP_WIDTH)
        cg = acc[:, gsl]
        mu = jnp.mean(cg, axis=-1, keepdims=True)
        dlt = cg - mu
        var = jnp.mean(dlt * dlt, axis=-1, keepdims=True)
        y = dlt * lax.rsqrt(var + EPS) * lg_ref[:, gsl] + lb_ref[:, gsl]
        gb = gb_all[:, gsl]
        mix_ref[:, GLA_WIDTH + gi * CONV_GROUP_WIDTH:GLA_WIDTH + (gi + 1) * CONV_GROUP_WIDTH] = (
            (y * _sigmoid(y)) * (gb * _sigmoid(gb))).astype(BF16)


def _output_kernel(h_ref, mix_ref, p_ref, wout_ref, pn_ref, wg_ref, bg_ref, wp_ref, fn_ref, o_ref, *,
                   apply_final_norm):
    h = h_ref[...] + _dot(mix_ref[...], wout_ref[...])
    ms = jnp.mean(h * h, axis=-1, keepdims=True)
    n = (h * lax.rsqrt(ms + EPS) * pn_ref[...]).astype(BF16)
    gate = _sigmoid(_dot(n, wg_ref[...]) + bg_ref[...])
    h = h + gate * _dot(p_ref[...].astype(BF16), wp_ref[...])
    if apply_final_norm:
        ms2 = jnp.mean(h * h, axis=-1, keepdims=True)
        h = h * lax.rsqrt(ms2 + EPS) * fn_ref[...]
    o_ref[...] = h


def _win_relayout_kernel(w_ref, o_ref):
    w = w_ref[...]
    rows = w.shape[0]
    o_ref[:, O_Q:O_CVAL] = w[:, :V_END].astype(BF16)
    o_ref[:, O_CVAL:O_GA] = w[:, GA_END:CGATE_END].astype(BF16)
    o_ref[:, O_GA:O_GB] = w[:, ALOW_END:GA_END].astype(BF16)
    o_ref[:, O_GB:O_ALOW] = w[:, CGATE_END:D_IN].astype(BF16)
    o_ref[:, O_ALOW:D_IN_PAD] = jnp.concatenate(
        [w[:, V_END:ALOW_END], jnp.zeros((rows, LANES - GATE_RANK), F32)], axis=1).astype(BF16)


def _resident(shape):
    nd = len(shape)
    return pl.BlockSpec(shape, lambda *_: (0,) * nd, pipeline_mode=pl.Buffered(1))


def _win_relayout_call(w_in):
    return pl.pallas_call(
        _win_relayout_kernel,
        out_shape=jax.ShapeDtypeStruct((D_MODEL, D_IN_PAD), BF16),
        grid=(D_MODEL // PREP_ROWS,),
        in_specs=[pl.BlockSpec((PREP_ROWS, D_IN), lambda i: (i, 0))],
        out_specs=pl.BlockSpec((PREP_ROWS, D_IN_PAD), lambda i: (i, 0)),
        compiler_params=pltpu.CompilerParams(dimension_semantics=("arbitrary",)),
        name="win_relayout",
    )(w_in)


def _mixer_call(h2d, seq, norm_mix, w_in_r, w_alpha_p, b_alpha, gla_norm, conv_w, conv_b, ln_g, ln_b,
                tril, lvl):
    tokens = h2d.shape[0]
    ts = SEQ_TILE
    row = lambda a: a.reshape(1, -1)
    args = (h2d, row(norm_mix), w_in_r, w_alpha_p, row(b_alpha), row(gla_norm), conv_w, row(conv_b),
            row(ln_g), row(ln_b), tril, lvl)
    in_specs = [pl.BlockSpec((ts, D_MODEL), lambda t: (t, 0))]
    in_specs += [_resident(a.shape) for a in args[1:]]
    return pl.pallas_call(
        functools.partial(_mixer_kernel, tiles_per_seq=seq // ts),
        out_shape=jax.ShapeDtypeStruct((tokens, D_MODEL), BF16),
        grid=(tokens // ts,),
        in_specs=in_specs,
        out_specs=pl.BlockSpec((ts, D_MODEL), lambda t: (t, 0)),
        scratch_shapes=[
            pltpu.VMEM((GLA_HEADS, GLA_DV, GLA_DK), F32),
            pltpu.VMEM((CONV_HIST + ts, CONV_WIDTH), F32),
        ],
        compiler_params=pltpu.CompilerParams(
            dimension_semantics=("arbitrary",), vmem_limit_bytes=VMEM_LIMIT),
        name="gla_conv_mixer",
    )(*args)


def _output_call(h2d, mix2d, p2d, w_out, ple_norm, w_gate, b_gate, w_ple, final_norm, apply_final_norm):
    tokens = h2d.shape[0]
    tm = TOKEN_TILE
    row = lambda a: a.reshape(1, -1)
    tile = lambda width: pl.BlockSpec((tm, width), lambda i: (i, 0))
    weights = (w_out, row(ple_norm), w_gate, row(b_gate), w_ple, row(final_norm))
    return pl.pallas_call(
        functools.partial(_output_kernel, apply_final_norm=apply_final_norm),
        out_shape=jax.ShapeDtypeStruct((tokens, D_MODEL), F32),
        grid=(tokens // tm,),
        in_specs=[tile(D_MODEL), tile(D_MODEL), tile(PLE_DIM)] + [_resident(a.shape) for a in weights],
        out_specs=tile(D_MODEL),
        compiler_params=pltpu.CompilerParams(
            dimension_semantics=("arbitrary",), vmem_limit_bytes=VMEM_LIMIT),
        name="outproj_ple",
    )(h2d, mix2d, p2d, *weights)


def kernel(x, p, norm_mix, w_in, w_alpha, b_alpha, gla_norm, conv_w, conv_b, conv_ln_g, conv_ln_b,
           w_out, ple_norm, w_ple_gate, b_ple_gate, w_ple, final_norm):
    bsz, seq, _ = x.shape
    depth = p.shape[0]
    tokens = bsz * seq
    assert seq % SEQ_TILE == 0 and tokens % TOKEN_TILE == 0
    tril = jnp.asarray(np.tril(np.ones((SEQ_TILE, SEQ_TILE), np.float32)), BF16)
    lvl = jnp.asarray(_level_map(SEQ_TILE))

    h = x.reshape(tokens, D_MODEL)
    for i in range(depth):
        w_alpha_p = jnp.concatenate(
            [w_alpha[i], jnp.zeros((LANES - GATE_RANK, GLA_KEY_WIDTH), w_alpha.dtype)], axis=0).astype(BF16)
        conv_w_p = jnp.concatenate([conv_w[i], jnp.zeros((1, CONV_WIDTH), conv_w.dtype)], axis=0)
        mix = _mixer_call(h, seq, norm_mix[i], _win_relayout_call(w_in[i]), w_alpha_p, b_alpha[i],
                          gla_norm[i], conv_w_p, conv_b[i], conv_ln_g[i], conv_ln_b[i], tril, lvl)
        h = _output_call(
            h, mix, p[i].reshape(tokens, PLE_DIM), w_out[i].astype(BF16), ple_norm[i],
            w_ple_gate[i].astype(BF16), b_ple_gate[i], w_ple[i].astype(BF16), final_norm,
            apply_final_norm=(i == depth - 1))
    return h.reshape(bsz, seq, D_MODEL)
```

```python
import functools
import math

import numpy as np
import jax
import jax.numpy as jnp
from jax import lax
from jax.experimental import pallas as pl
from jax.experimental.pallas import tpu as pltpu

F32 = jnp.float32
BF16 = jnp.bfloat16

D_MODEL = 2048
PLE_DIM = 256
GLA_WIDTH = 1024
CONV_WIDTH = 1024
GLA_HEADS = 4
GLA_DV = GLA_WIDTH // GLA_HEADS
GLA_DK = GLA_DV // 2
GLA_KEY_WIDTH = GLA_HEADS * GLA_DK
GATE_RANK = 16
GATE_TAU = 16.0
CONV_K = 31
CONV_GROUPS = 8
CONV_GROUP_WIDTH = CONV_WIDTH // CONV_GROUPS
EPS = 1e-6
LOG2E = math.log2(math.e)

SUBLANES = 8
LANES = 128

Q_END = GLA_KEY_WIDTH
K_END = Q_END + GLA_KEY_WIDTH
V_END = K_END + GLA_WIDTH
ALOW_END = V_END + GATE_RANK
GA_END = ALOW_END + GLA_WIDTH
CVAL_END = GA_END + CONV_WIDTH
CGATE_END = CVAL_END + CONV_WIDTH
D_IN = CGATE_END + CONV_WIDTH

RELAYOUT_COLS = 256
ROW_ALIGN = 16

O_Q = 0
O_K = O_Q + GLA_KEY_WIDTH
O_V = O_K + GLA_KEY_WIDTH
O_CVAL = O_V + GLA_WIDTH
O_CGATE = O_CVAL + CONV_WIDTH
O_GA = O_CGATE + CONV_WIDTH
O_GB = O_GA + GLA_WIDTH
O_ALOW = O_GB + CONV_WIDTH
D_IN_PAD = O_ALOW + RELAYOUT_COLS

SEQ_TILE = 256
TOKEN_TILE = 512
CONV_HIST = 32
DIAG_LEVEL = 0
VMEM_LIMIT = 56 * 1024 * 1024


def _sigmoid(x):
    return 1.0 / (1.0 + jnp.exp(-x))


def _dot(a, b):
    return jnp.dot(a, b, preferred_element_type=F32)


def _dot_nt(a, b):
    return lax.dot_general(a, b, (((1,), (1,)), ((), ())), preferred_element_type=F32)


def _dot_tn(a, b):
    return lax.dot_general(a, b, (((0,), (0,)), ((), ())), preferred_element_type=F32)


def _level_sizes(ts):
    sizes = []
    s = ts // 2
    while s >= SUBLANES:
        sizes.append(s)
        s //= 2
    return tuple(sizes)


def _level_map(ts):
    i = np.arange(ts)[:, None]
    j = np.arange(ts)[None, :]
    x = np.bitwise_xor(i, j)
    top = np.floor(np.log2(np.maximum(x, 1))).astype(np.int32)
    lvl = np.where(j > i, -1, np.where(i // SUBLANES == j // SUBLANES, DIAG_LEVEL, top))
    return lvl.astype(np.int32)


def _mixer_kernel(x_ref, nm_ref, win_ref, wal_ref, bal_ref, gn_ref, cw_ref, cb_ref, lg_ref, lb_ref,
                  tril_ref, lvl_ref, mix_ref, state_ref, cbuf_ref, *, tiles_per_seq):
    ts = SEQ_TILE
    nblk = ts // SUBLANES

    @pl.when(lax.rem(pl.program_id(0), tiles_per_seq) == 0)
    def _():
        state_ref[...] = jnp.zeros_like(state_ref)
        cbuf_ref[0:CONV_HIST, :] = jnp.zeros((CONV_HIST, CONV_WIDTH), F32)

    x = x_ref[...]
    ms = jnp.mean(x * x, axis=-1, keepdims=True)
    u = (x * lax.rsqrt(ms + EPS) * nm_ref[...]).astype(BF16)

    def proj(lo, width):
        return _dot(u, win_ref[:, lo:lo + width])

    glu = proj(O_CVAL, O_GA - O_CVAL)
    cbuf_ref[CONV_HIST:CONV_HIST + ts, :] = glu[:, :CONV_WIDTH] * _sigmoid(glu[:, CONV_WIDTH:])
    acc = jnp.broadcast_to(cb_ref[...], (ts, CONV_WIDTH))
    for r in range(SUBLANES):
        v_r = None
        for a in range((CONV_K - 1 - r) // SUBLANES + 1):
            tap = CONV_K - 1 - (SUBLANES * a + r)
            lo = CONV_HIST - SUBLANES * (a + 1)
            term = cw_ref[tap:tap + 1, :] * cbuf_ref[lo:lo + ts + SUBLANES, :]
            v_r = term if v_r is None else v_r + term
        shifted = v_r if r == 0 else pltpu.roll(v_r, r, axis=0)
        acc = acc + shifted[SUBLANES:SUBLANES + ts]
    cbuf_ref[0:CONV_HIST, :] = cbuf_ref[ts:ts + CONV_HIST, :]

    a_low = proj(O_ALOW, LANES).astype(BF16)
    z = _dot(a_low, wal_ref[...]) + bal_ref[...]
    g = (jnp.minimum(z, 0.0) - jnp.log1p(jnp.exp(-jnp.abs(z)))) * (1.0 / GATE_TAU)
    g_hi = g.astype(BF16)
    r1 = g - g_hi.astype(F32)
    g_mid = r1.astype(BF16)
    g_lo = (r1 - g_mid.astype(F32)).astype(BF16)
    tril3 = jnp.concatenate([tril_ref[...]] * 3, axis=1)
    b_all = _dot(tril3, jnp.concatenate([g_hi, g_mid, g_lo], axis=0)) * LOG2E

    qkv = proj(O_Q, O_CVAL - O_Q)
    q_all = qkv[:, O_Q:O_K] * (GLA_DK ** -0.5)
    k_all = qkv[:, O_K:O_V]
    v_all = qkv[:, O_V:O_CVAL].astype(BF16)
    ga_all = proj(O_GA, GLA_WIDTH)
    lvl = lvl_ref[...]
    lane_mod = lax.broadcasted_iota(jnp.int32, (nblk, SUBLANES, LANES), 2) & (SUBLANES - 1)

    for h in range(GLA_HEADS):
        ksl = slice(h * GLA_DK, (h + 1) * GLA_DK)
        vsl = slice(h * GLA_DV, (h + 1) * GLA_DV)
        q = q_all[:, ksl]
        k = k_all[:, ksl]
        b = b_all[:, ksl]
        v = v_all[:, vsl]

        p = jnp.zeros((ts, ts), F32)
        for s in _level_sizes(ts):
            b3 = b.reshape(ts // (2 * s), 2 * s, GLA_DK)
            a = jnp.exp2(-jnp.abs(b3 - b3[:, s - 1:s, :])).reshape(ts, GLA_DK)
            p_l = _dot_nt((q * a).astype(BF16), (k * a).astype(BF16))
            p = jnp.where(lvl == int(np.log2(s)), p_l, p)

        b3 = b.reshape(nblk, SUBLANES, GLA_DK)
        q3 = q.reshape(nblk, SUBLANES, GLA_DK)
        k3 = k.reshape(nblk, SUBLANES, GLA_DK)
        compact = jnp.zeros((nblk, SUBLANES, LANES), F32)
        for d in range(SUBLANES):
            t = q3 * jnp.exp2(b3 - b3[:, d:d + 1, :]) * k3[:, d:d + 1, :]
            compact = jnp.where(lane_mod == d, jnp.sum(t, axis=-1, keepdims=True), compact)
        compact = compact.reshape(ts, LANES)
        p = jnp.where(lvl == DIAG_LEVEL, jnp.concatenate([compact] * (ts // LANES), axis=1), p)

        st = state_ref[h]
        b_last = b[ts - 1:ts, :]
        o = _dot(p.astype(BF16), v) + _dot_nt((q * jnp.exp2(b)).astype(BF16), st.astype(BF16))
        k_hat = (k * jnp.exp2(b_last - b)).astype(BF16)
        state_ref[h] = st * jnp.exp2(b_last) + _dot_tn(v, k_hat)

        o_ms = jnp.mean(o * o, axis=-1, keepdims=True)
        o_n = o * lax.rsqrt(o_ms + EPS) * gn_ref[...]
        ga = ga_all[:, vsl]
        mix_ref[:, vsl] = (o_n * (ga * _sigmoid(ga))).astype(BF16)

    gb_all = proj(O_GB, CONV_WIDTH)
    for gi in range(CONV_GROUPS):
        gsl = slice(gi * CONV_GROUP_WIDTH, (gi + 1) * CONV_GROUP_WIDTH)
        cg = acc[:, gsl]
        mu = jnp.mean(cg, axis=-1, keepdims=True)
        dlt = cg - mu
        var = jnp.mean(dlt * dlt, axis=-1, keepdims=True)
        y = dlt * lax.rsqrt(var + EPS) * lg_ref[:, gsl] + lb_ref[:, gsl]
        gb = gb_all[:, gsl]
        mix_ref[:, GLA_WIDTH + gi * CONV_GROUP_WIDTH:GLA_WIDTH + (gi + 1) * CONV_GROUP_WIDTH] = (
            (y * _sigmoid(y)) * (gb * _sigmoid(gb))).astype(BF16)


def _output_kernel(h_ref, mix_ref, p_ref, wout_ref, pn_ref, wg_ref, bg_ref, wp_ref, fn_ref, o_ref, *,
                   apply_final_norm):
    h = h_ref[...] + _dot(mix_ref[...], wout_ref[...])
    ms = jnp.mean(h * h, axis=-1, keepdims=True)
    n = (h * lax.rsqrt(ms + EPS) * pn_ref[...]).astype(BF16)
    gate = _sigmoid(_dot(n, wg_ref[...]) + bg_ref[...])
    h = h + gate * _dot(p_ref[...].astype(BF16), wp_ref[...])
    if apply_final_norm:
        ms2 = jnp.mean(h * h, axis=-1, keepdims=True)
        h = h * lax.rsqrt(ms2 + EPS) * fn_ref[...]
    o_ref[...] = h


def _win_relayout_kernel(src_rows_ref, wt_ref, o_ref):
    del src_rows_ref
    o_ref[...] = wt_ref[...].T.astype(BF16)


def _resident(shape):
    nd = len(shape)
    return pl.BlockSpec(shape, lambda *_: (0,) * nd, pipeline_mode=pl.Buffered(1))


def _win_relayout_call(w_in_t):
    segments = ((O_Q, 0, V_END), (O_CVAL, GA_END, CGATE_END - GA_END), (O_GA, ALOW_END, GLA_WIDTH),
                (O_GB, CGATE_END, CONV_WIDTH), (O_ALOW, V_END, RELAYOUT_COLS))
    src_rows = np.zeros((D_IN_PAD // RELAYOUT_COLS,), np.int32)
    for dst, src, width in segments:
        for off in range(0, width, RELAYOUT_COLS):
            assert (src + off) % ROW_ALIGN == 0 and src + off + RELAYOUT_COLS <= D_IN
            src_rows[(dst + off) // RELAYOUT_COLS] = (src + off) // ROW_ALIGN
    return pl.pallas_call(
        _win_relayout_kernel,
        out_shape=jax.ShapeDtypeStruct((D_MODEL, D_IN_PAD), BF16),
        grid_spec=pltpu.PrefetchScalarGridSpec(
            num_scalar_prefetch=1,
            grid=(D_IN_PAD // RELAYOUT_COLS,),
            in_specs=[pl.BlockSpec((pl.Element(RELAYOUT_COLS), pl.Element(D_MODEL)),
                                   lambda j, rows: (rows[j] * ROW_ALIGN, 0))],
            out_specs=pl.BlockSpec((D_MODEL, RELAYOUT_COLS), lambda j, rows: (0, j)),
        ),
        compiler_params=pltpu.CompilerParams(dimension_semantics=("arbitrary",)),
        name="win_relayout",
    )(jnp.asarray(src_rows), w_in_t)


def _mixer_call(h2d, seq, norm_mix, w_in_r, w_alpha_p, b_alpha, gla_norm, conv_w, conv_b, ln_g, ln_b,
                tril, lvl):
    tokens = h2d.shape[0]
    ts = SEQ_TILE
    row = lambda a: a.reshape(1, -1)
    args = (h2d, row(norm_mix), w_in_r, w_alpha_p, row(b_alpha), row(gla_norm), conv_w, row(conv_b),
            row(ln_g), row(ln_b), tril, lvl)
    in_specs = [pl.BlockSpec((ts, D_MODEL), lambda t: (t, 0))]
    in_specs += [_resident(a.shape) for a in args[1:]]
    return pl.pallas_call(
        functools.partial(_mixer_kernel, tiles_per_seq=seq // ts),
        out_shape=jax.ShapeDtypeStruct((tokens, D_MODEL), BF16),
        grid=(tokens // ts,),
        in_specs=in_specs,
        out_specs=pl.BlockSpec((ts, D_MODEL), lambda t: (t, 0)),
        scratch_shapes=[
            pltpu.VMEM((GLA_HEADS, GLA_DV, GLA_DK), F32),
            pltpu.VMEM((CONV_HIST + ts, CONV_WIDTH), F32),
        ],
        compiler_params=pltpu.CompilerParams(
            dimension_semantics=("arbitrary",), vmem_limit_bytes=VMEM_LIMIT),
        name="gla_conv_mixer",
    )(*args)


def _output_call(h2d, mix2d, p2d, w_out, ple_norm, w_gate, b_gate, w_ple, final_norm, apply_final_norm):
    tokens = h2d.shape[0]
    tm = TOKEN_TILE
    row = lambda a: a.reshape(1, -1)
    tile = lambda width: pl.BlockSpec((tm, width), lambda i: (i, 0))
    weights = (w_out, row(ple_norm), w_gate, row(b_gate), w_ple, row(final_norm))
    return pl.pallas_call(
        functools.partial(_output_kernel, apply_final_norm=apply_final_norm),
        out_shape=jax.ShapeDtypeStruct((tokens, D_MODEL), F32),
        grid=(tokens // tm,),
        in_specs=[tile(D_MODEL), tile(D_MODEL), tile(PLE_DIM)] + [_resident(a.shape) for a in weights],
        out_specs=tile(D_MODEL),
        compiler_params=pltpu.CompilerParams(
            dimension_semantics=("arbitrary",), vmem_limit_bytes=VMEM_LIMIT),
        name="outproj_ple",
    )(h2d, mix2d, p2d, *weights)


def kernel(x, p, norm_mix, w_in, w_alpha, b_alpha, gla_norm, conv_w, conv_b, conv_ln_g, conv_ln_b,
           w_out, ple_norm, w_ple_gate, b_ple_gate, w_ple, final_norm):
    bsz, seq, _ = x.shape
    depth = p.shape[0]
    tokens = bsz * seq
    assert seq % SEQ_TILE == 0 and tokens % TOKEN_TILE == 0
    tril = jnp.asarray(np.tril(np.ones((SEQ_TILE, SEQ_TILE), np.float32)), BF16)
    lvl = jnp.asarray(_level_map(SEQ_TILE))

    h = x.reshape(tokens, D_MODEL)
    for i in range(depth):
        w_alpha_p = jnp.concatenate(
            [w_alpha[i], jnp.zeros((LANES - GATE_RANK, GLA_KEY_WIDTH), w_alpha.dtype)], axis=0).astype(BF16)
        conv_w_p = jnp.concatenate([conv_w[i], jnp.zeros((1, CONV_WIDTH), conv_w.dtype)], axis=0)
        w_in_t = jnp.swapaxes(w_in[i], 0, 1)
        mix = _mixer_call(h, seq, norm_mix[i], _win_relayout_call(w_in_t), w_alpha_p, b_alpha[i],
                          gla_norm[i], conv_w_p, conv_b[i], conv_ln_g[i], conv_ln_b[i], tril, lvl)
        h = _output_call(
            h, mix, p[i].reshape(tokens, PLE_DIM), w_out[i].astype(BF16), ple_norm[i],
            w_ple_gate[i].astype(BF16), b_ple_gate[i], w_ple[i].astype(BF16), final_norm,
            apply_final_norm=(i == depth - 1))
    return h.reshape(bsz, seq, D_MODEL)
```

```python
import functools
import math

import numpy as np
import jax
import jax.numpy as jnp
from jax import lax
from jax.experimental import pallas as pl
from jax.experimental.pallas import tpu as pltpu

F32 = jnp.float32
BF16 = jnp.bfloat16

D_MODEL = 2048
PLE_DIM = 256
GLA_WIDTH = 1024
CONV_WIDTH = 1024
GLA_HEADS = 4
GLA_DV = GLA_WIDTH // GLA_HEADS
GLA_DK = GLA_DV // 2
GLA_KEY_WIDTH = GLA_HEADS * GLA_DK
GATE_RANK = 16
GATE_TAU = 16.0
CONV_K = 31
CONV_GROUPS = 8
CONV_GROUP_WIDTH = CONV_WIDTH // CONV_GROUPS
EPS = 1e-6
LOG2E = math.log2(math.e)

SUBLANES = 8
LANES = 128

Q_END = GLA_KEY_WIDTH
K_END = Q_END + GLA_KEY_WIDTH
V_END = K_END + GLA_WIDTH
ALOW_END = V_END + GATE_RANK
GA_END = ALOW_END + GLA_WIDTH
CVAL_END = GA_END + CONV_WIDTH
CGATE_END = CVAL_END + CONV_WIDTH
D_IN = CGATE_END + CONV_WIDTH

RELAYOUT_COLS = 256
ROW_ALIGN = 16
CONV_PARTS = 4
CONV_PART = 1024 // CONV_PARTS

O_Q = 0
O_K = O_Q + GLA_KEY_WIDTH
O_V = O_K + GLA_KEY_WIDTH
O_GLU = O_V + GLA_WIDTH
O_GA = O_GLU + 2 * CONV_WIDTH
O_GB = O_GA + GLA_WIDTH
O_ALOW = O_GB + CONV_WIDTH
D_IN_PAD = O_ALOW + RELAYOUT_COLS

SEQ_TILE = 256
TOKEN_TILE = 512
CONV_HIST = 32
DIAG_LEVEL = 0
VMEM_LIMIT = 56 * 1024 * 1024


def _sigmoid(x):
    return 1.0 / (1.0 + jnp.exp(-x))


def _dot(a, b):
    return jnp.dot(a, b, preferred_element_type=F32)


def _dot_nt(a, b):
    return lax.dot_general(a, b, (((1,), (1,)), ((), ())), preferred_element_type=F32)


def _dot_tn(a, b):
    return lax.dot_general(a, b, (((0,), (0,)), ((), ())), preferred_element_type=F32)


def _level_sizes(ts):
    sizes = []
    s = ts // 2
    while s >= SUBLANES:
        sizes.append(s)
        s //= 2
    return tuple(sizes)


def _level_map(ts):
    i = np.arange(ts)[:, None]
    j = np.arange(ts)[None, :]
    x = np.bitwise_xor(i, j)
    top = np.floor(np.log2(np.maximum(x, 1))).astype(np.int32)
    lvl = np.where(j > i, -1, np.where(i // SUBLANES == j // SUBLANES, DIAG_LEVEL, top))
    return lvl.astype(np.int32)


def _mixer_kernel(x_ref, nm_ref, win_ref, wal_ref, bal_ref, gn_ref, cw_ref, cb_ref, lg_ref, lb_ref,
                  tril_ref, lvl_ref, mix_ref, state_ref, cbuf_ref, *, tiles_per_seq):
    ts = SEQ_TILE
    nblk = ts // SUBLANES

    @pl.when(lax.rem(pl.program_id(0), tiles_per_seq) == 0)
    def _():
        state_ref[...] = jnp.zeros_like(state_ref)
        cbuf_ref[0:CONV_HIST, :] = jnp.zeros((CONV_HIST, CONV_WIDTH), F32)

    x = x_ref[...]
    ms = jnp.mean(x * x, axis=-1, keepdims=True)
    u = (x * lax.rsqrt(ms + EPS) * nm_ref[...]).astype(BF16)

    def proj(lo, width):
        return _dot(u, win_ref[:, lo:lo + width])

    acc_parts = []
    for part in range(CONV_PARTS):
        csl = slice(part * CONV_PART, (part + 1) * CONV_PART)
        glu = proj(O_GLU + 2 * part * CONV_PART, 2 * CONV_PART)
        cbuf_ref[CONV_HIST:CONV_HIST + ts, csl] = glu[:, :CONV_PART] * _sigmoid(glu[:, CONV_PART:])
        acc = jnp.broadcast_to(cb_ref[:, csl], (ts, CONV_PART))
        for r in range(SUBLANES):
            v_r = None
            for a in range((CONV_K - 1 - r) // SUBLANES + 1):
                tap = CONV_K - 1 - (SUBLANES * a + r)
                lo = CONV_HIST - SUBLANES * (a + 1)
                term = cw_ref[tap:tap + 1, csl] * cbuf_ref[lo:lo + ts + SUBLANES, csl]
                v_r = term if v_r is None else v_r + term
            shifted = v_r if r == 0 else pltpu.roll(v_r, r, axis=0)
            acc = acc + shifted[SUBLANES:SUBLANES + ts]
        cbuf_ref[0:CONV_HIST, csl] = cbuf_ref[ts:ts + CONV_HIST, csl]
        acc_parts.append(acc)
    acc = jnp.concatenate(acc_parts, axis=1)

    a_low = proj(O_ALOW, LANES).astype(BF16)
    z = _dot(a_low, wal_ref[...]) + bal_ref[...]
    g = (jnp.minimum(z, 0.0) - jnp.log1p(jnp.exp(-jnp.abs(z)))) * (1.0 / GATE_TAU)
    g_hi = g.astype(BF16)
    r1 = g - g_hi.astype(F32)
    g_mid = r1.astype(BF16)
    g_lo = (r1 - g_mid.astype(F32)).astype(BF16)
    tril3 = jnp.concatenate([tril_ref[...]] * 3, axis=1)
    b_all = _dot(tril3, jnp.concatenate([g_hi, g_mid, g_lo], axis=0)) * LOG2E

    qkv = proj(O_Q, O_GLU - O_Q)
    q_all = qkv[:, O_Q:O_K] * (GLA_DK ** -0.5)
    k_all = qkv[:, O_K:O_V]
    v_all = qkv[:, O_V:O_GLU].astype(BF16)
    ga_all = proj(O_GA, GLA_WIDTH)
    lvl = lvl_ref[...]
    lane_mod = lax.broadcasted_iota(jnp.int32, (nblk, SUBLANES, LANES), 2) & (SUBLANES - 1)

    for h in range(GLA_HEADS):
        ksl = slice(h * GLA_DK, (h + 1) * GLA_DK)
        vsl = slice(h * GLA_DV, (h + 1) * GLA_DV)
        q = q_all[:, ksl]
        k = k_all[:, ksl]
        b = b_all[:, ksl]
        v = v_all[:, vsl]

        p = jnp.zeros((ts, ts), BF16)
        for s in _level_sizes(ts):
            b3 = b.reshape(ts // (2 * s), 2 * s, GLA_DK)
            a = jnp.exp2(-jnp.abs(b3 - b3[:, s - 1:s, :])).reshape(ts, GLA_DK)
            p_l = _dot_nt((q * a).astype(BF16), (k * a).astype(BF16))
            p = jnp.where(lvl == float(np.log2(s)), p_l.astype(BF16), p)

        b3 = b.reshape(nblk, SUBLANES, GLA_DK)
        q3 = q.reshape(nblk, SUBLANES, GLA_DK)
        k3 = k.reshape(nblk, SUBLANES, GLA_DK)
        compact = jnp.zeros((nblk, SUBLANES, LANES), F32)
        for d in range(SUBLANES):
            t = q3 * jnp.exp2(b3 - b3[:, d:d + 1, :]) * k3[:, d:d + 1, :]
            compact = jnp.where(lane_mod == d, jnp.sum(t, axis=-1, keepdims=True), compact)
        compact = compact.reshape(ts, LANES)
        p = jnp.where(lvl == float(DIAG_LEVEL),
                      jnp.concatenate([compact.astype(BF16)] * (ts // LANES), axis=1), p)

        st = state_ref[h]
        b_last = b[ts - 1:ts, :]
        o = _dot(p, v) + _dot_nt((q * jnp.exp2(b)).astype(BF16), st.astype(BF16))
        k_hat = (k * jnp.exp2(b_last - b)).astype(BF16)
        state_ref[h] = st * jnp.exp2(b_last) + _dot_tn(v, k_hat)

        o_ms = jnp.mean(o * o, axis=-1, keepdims=True)
        o_n = o * lax.rsqrt(o_ms + EPS) * gn_ref[...]
        ga = ga_all[:, vsl]
        mix_ref[:, vsl] = (o_n * (ga * _sigmoid(ga))).astype(BF16)

    gb_all = proj(O_GB, CONV_WIDTH)
    for gi in range(CONV_GROUPS):
        gsl = slice(gi * CONV_GROUP_WIDTH, (gi + 1) * CONV_GROUP_WIDTH)
        cg = acc[:, gsl]
        mu = jnp.mean(cg, axis=-1, keepdims=True)
        dlt = cg - mu
        var = jnp.mean(dlt * dlt, axis=-1, keepdims=True)
        y = dlt * lax.rsqrt(var + EPS) * lg_ref[:, gsl] + lb_ref[:, gsl]
        gb = gb_all[:, gsl]
        mix_ref[:, GLA_WIDTH + gi * CONV_GROUP_WIDTH:GLA_WIDTH + (gi + 1) * CONV_GROUP_WIDTH] = (
            (y * _sigmoid(y)) * (gb * _sigmoid(gb))).astype(BF16)


def _output_kernel(h_ref, mix_ref, p_ref, wout_ref, pn_ref, wg_ref, bg_ref, wp_ref, fn_ref, o_ref, *,
                   apply_final_norm):
    h = h_ref[...] + _dot(mix_ref[...], wout_ref[...])
    ms = jnp.mean(h * h, axis=-1, keepdims=True)
    n = (h * lax.rsqrt(ms + EPS) * pn_ref[...]).astype(BF16)
    gate = _sigmoid(_dot(n, wg_ref[...]) + bg_ref[...])
    h = h + gate * _dot(p_ref[...].astype(BF16), wp_ref[...])
    if apply_final_norm:
        ms2 = jnp.mean(h * h, axis=-1, keepdims=True)
        h = h * lax.rsqrt(ms2 + EPS) * fn_ref[...]
    o_ref[...] = h


def _win_relayout_kernel(src_rows_ref, wt_ref, o_ref):
    del src_rows_ref
    o_ref[...] = wt_ref[...].T.astype(BF16)


def _resident(shape):
    nd = len(shape)
    return pl.BlockSpec(shape, lambda *_: (0,) * nd, pipeline_mode=pl.Buffered(1))


def _win_relayout_call(w_in_t):
    segments = [(O_Q, 0, V_END), (O_GA, ALOW_END, GLA_WIDTH), (O_GB, CGATE_END, CONV_WIDTH),
                (O_ALOW, V_END, RELAYOUT_COLS)]
    for part in range(CONV_PARTS):
        segments.append((O_GLU + 2 * part * CONV_PART, GA_END + part * CONV_PART, CONV_PART))
        segments.append((O_GLU + (2 * part + 1) * CONV_PART, CVAL_END + part * CONV_PART, CONV_PART))
    src_rows = np.zeros((D_IN_PAD // RELAYOUT_COLS,), np.int32)
    for dst, src, width in segments:
        for off in range(0, width, RELAYOUT_COLS):
            assert (src + off) % ROW_ALIGN == 0 and src + off + RELAYOUT_COLS <= D_IN
            src_rows[(dst + off) // RELAYOUT_COLS] = (src + off) // ROW_ALIGN
    return pl.pallas_call(
        _win_relayout_kernel,
        out_shape=jax.ShapeDtypeStruct((D_MODEL, D_IN_PAD), BF16),
        grid_spec=pltpu.PrefetchScalarGridSpec(
            num_scalar_prefetch=1,
            grid=(D_IN_PAD // RELAYOUT_COLS,),
            in_specs=[pl.BlockSpec((pl.Element(RELAYOUT_COLS), pl.Element(D_MODEL)),
                                   lambda j, rows: (rows[j] * ROW_ALIGN, 0))],
            out_specs=pl.BlockSpec((D_MODEL, RELAYOUT_COLS), lambda j, rows: (0, j)),
        ),
        compiler_params=pltpu.CompilerParams(dimension_semantics=("arbitrary",)),
        name="win_relayout",
    )(jnp.asarray(src_rows), w_in_t)


def _mixer_call(h2d, seq, norm_mix, w_in_r, w_alpha_p, b_alpha, gla_norm, conv_w, conv_b, ln_g, ln_b,
                tril, lvl):
    tokens = h2d.shape[0]
    ts = SEQ_TILE
    row = lambda a: a.reshape(1, -1)
    args = (h2d, row(norm_mix), w_in_r, w_alpha_p, row(b_alpha), row(gla_norm), conv_w, row(conv_b),
            row(ln_g), row(ln_b), tril, lvl)
    in_specs = [pl.BlockSpec((ts, D_MODEL), lambda t: (t, 0))]
    in_specs += [_resident(a.shape) for a in args[1:]]
    return pl.pallas_call(
        functools.partial(_mixer_kernel, tiles_per_seq=seq // ts),
        out_shape=jax.ShapeDtypeStruct((tokens, D_MODEL), BF16),
        grid=(tokens // ts,),
        in_specs=in_specs,
        out_specs=pl.BlockSpec((ts, D_MODEL), lambda t: (t, 0)),
        scratch_shapes=[
            pltpu.VMEM((GLA_HEADS, GLA_DV, GLA_DK), F32),
            pltpu.VMEM((CONV_HIST + ts, CONV_WIDTH), F32),
        ],
        compiler_params=pltpu.CompilerParams(
            dimension_semantics=("arbitrary",), vmem_limit_bytes=VMEM_LIMIT),
        name="gla_conv_mixer",
    )(*args)


def _output_call(h2d, mix2d, p2d, w_out, ple_norm, w_gate, b_gate, w_ple, final_norm, apply_final_norm):
    tokens = h2d.shape[0]
    tm = TOKEN_TILE
    row = lambda a: a.reshape(1, -1)
    tile = lambda width: pl.BlockSpec((tm, width), lambda i: (i, 0))
    weights = (w_out, row(ple_norm), w_gate, row(b_gate), w_ple, row(final_norm))
    return pl.pallas_call(
        functools.partial(_output_kernel, apply_final_norm=apply_final_norm),
        out_shape=jax.ShapeDtypeStruct((tokens, D_MODEL), F32),
        grid=(tokens // tm,),
        in_specs=[tile(D_MODEL), tile(D_MODEL), tile(PLE_DIM)] + [_resident(a.shape) for a in weights],
        out_specs=tile(D_MODEL),
        compiler_params=pltpu.CompilerParams(
            dimension_semantics=("arbitrary",), vmem_limit_bytes=VMEM_LIMIT),
        name="outproj_ple",
    )(h2d, mix2d, p2d, *weights)


def kernel(x, p, norm_mix, w_in, w_alpha, b_alpha, gla_norm, conv_w, conv_b, conv_ln_g, conv_ln_b,
           w_out, ple_norm, w_ple_gate, b_ple_gate, w_ple, final_norm):
    bsz, seq, _ = x.shape
    depth = p.shape[0]
    tokens = bsz * seq
    assert seq % SEQ_TILE == 0 and tokens % TOKEN_TILE == 0
    tril = jnp.asarray(np.tril(np.ones((SEQ_TILE, SEQ_TILE), np.float32)), BF16)
    lvl = jnp.asarray(_level_map(SEQ_TILE), BF16)

    h = x.reshape(tokens, D_MODEL)
    for i in range(depth):
        w_alpha_p = jnp.concatenate(
            [w_alpha[i], jnp.zeros((LANES - GATE_RANK, GLA_KEY_WIDTH), w_alpha.dtype)], axis=0).astype(BF16)
        conv_w_p = jnp.concatenate([conv_w[i], jnp.zeros((1, CONV_WIDTH), conv_w.dtype)], axis=0)
        w_in_t = jnp.swapaxes(w_in[i], 0, 1)
        mix = _mixer_call(h, seq, norm_mix[i], _win_relayout_call(w_in_t), w_alpha_p, b_alpha[i],
                          gla_norm[i], conv_w_p, conv_b[i], conv_ln_g[i], conv_ln_b[i], tril, lvl)
        h = _output_call(
            h, mix, p[i].reshape(tokens, PLE_DIM), w_out[i].astype(BF16), ple_norm[i],
            w_ple_gate[i].astype(BF16), b_ple_gate[i], w_ple[i].astype(BF16), final_norm,
            apply_final_norm=(i == depth - 1))
    return h.reshape(bsz, seq, D_MODEL)
```

```python
import functools
import math

import numpy as np
import jax
import jax.numpy as jnp
from jax import lax
from jax.experimental import pallas as pl
from jax.experimental.pallas import tpu as pltpu

F32 = jnp.float32
BF16 = jnp.bfloat16

D_MODEL = 2048
PLE_DIM = 256
GLA_WIDTH = 1024
CONV_WIDTH = 1024
GLA_HEADS = 4
GLA_DV = GLA_WIDTH // GLA_HEADS
GLA_DK = GLA_DV // 2
GLA_KEY_WIDTH = GLA_HEADS * GLA_DK
GATE_RANK = 16
GATE_TAU = 16.0
CONV_K = 31
CONV_GROUPS = 8
CONV_GROUP_WIDTH = CONV_WIDTH // CONV_GROUPS
EPS = 1e-6
LOG2E = math.log2(math.e)

SUBLANES = 8
LANES = 128

Q_END = GLA_KEY_WIDTH
K_END = Q_END + GLA_KEY_WIDTH
V_END = K_END + GLA_WIDTH
ALOW_END = V_END + GATE_RANK
GA_END = ALOW_END + GLA_WIDTH
CVAL_END = GA_END + CONV_WIDTH
CGATE_END = CVAL_END + CONV_WIDTH
D_IN = CGATE_END + CONV_WIDTH

RELAYOUT_COLS = 256
ROW_ALIGN = 16
CONV_PARTS = 4
CONV_PART = 1024 // CONV_PARTS

O_Q = 0
O_K = O_Q + GLA_KEY_WIDTH
O_V = O_K + GLA_KEY_WIDTH
O_GLU = O_V + GLA_WIDTH
O_GA = O_GLU + 2 * CONV_WIDTH
O_GB = O_GA + GLA_WIDTH
O_ALOW = O_GB + CONV_WIDTH
D_IN_PAD = O_ALOW + RELAYOUT_COLS

SEQ_TILE = 256
TOKEN_TILE = 512
CONV_HIST = 32
DIAG_LEVEL = 0
VMEM_LIMIT = 56 * 1024 * 1024


def _sigmoid(x):
    return 1.0 / (1.0 + jnp.exp(-x))


def _dot(a, b):
    return jnp.dot(a, b, preferred_element_type=F32)


def _dot_nt(a, b):
    return lax.dot_general(a, b, (((1,), (1,)), ((), ())), preferred_element_type=F32)


def _dot_tn(a, b):
    return lax.dot_general(a, b, (((0,), (0,)), ((), ())), preferred_element_type=F32)


def _level_sizes(ts):
    sizes = []
    s = ts // 2
    while s >= SUBLANES:
        sizes.append(s)
        s //= 2
    return tuple(sizes)


def _level_map(ts):
    i = np.arange(ts)[:, None]
    j = np.arange(ts)[None, :]
    x = np.bitwise_xor(i, j)
    top = np.floor(np.log2(np.maximum(x, 1))).astype(np.int32)
    lvl = np.where(j > i, -1, np.where(i // SUBLANES == j // SUBLANES, DIAG_LEVEL, top))
    return lvl.astype(np.int32)


def _mixer_kernel(x_ref, nm_ref, win_ref, wal_ref, bal_ref, gn_ref, cw_ref, cb_ref, lg_ref, lb_ref,
                  tril_ref, lvl_ref, mix_ref, state_ref, cbuf_ref, *, tiles_per_seq):
    ts = SEQ_TILE
    nblk = ts // SUBLANES

    @pl.when(lax.rem(pl.program_id(0), tiles_per_seq) == 0)
    def _():
        state_ref[...] = jnp.zeros_like(state_ref)
        cbuf_ref[:, 0:CONV_HIST, :] = jnp.zeros((CONV_GROUPS, CONV_HIST, CONV_GROUP_WIDTH), F32)

    x = x_ref[...]
    ms = jnp.mean(x * x, axis=-1, keepdims=True)
    u = (x * lax.rsqrt(ms + EPS) * nm_ref[...]).astype(BF16)

    def proj(lo, width):
        return _dot(u, win_ref[:, lo:lo + width])

    acc_groups = []
    for part in range(CONV_PARTS):
        glu = proj(O_GLU + 2 * part * CONV_PART, 2 * CONV_PART)
        c = glu[:, :CONV_PART] * _sigmoid(glu[:, CONV_PART:])
        for sub in range(CONV_PART // CONV_GROUP_WIDTH):
            gi = part * (CONV_PART // CONV_GROUP_WIDTH) + sub
            gsl = slice(gi * CONV_GROUP_WIDTH, (gi + 1) * CONV_GROUP_WIDTH)
            cbuf_ref[gi, CONV_HIST:CONV_HIST + ts, :] = c[:, sub * CONV_GROUP_WIDTH:(sub + 1) * CONV_GROUP_WIDTH]
            acc = jnp.broadcast_to(cb_ref[:, gsl], (ts, CONV_GROUP_WIDTH))
            for tap in range(CONV_K):
                start = CONV_HIST - (CONV_K - 1) + tap
                acc = acc + cw_ref[tap:tap + 1, gsl] * cbuf_ref[gi, start:start + ts, :]
            cbuf_ref[gi, 0:CONV_HIST, :] = cbuf_ref[gi, ts:ts + CONV_HIST, :]
            acc_groups.append(acc)

    a_low = proj(O_ALOW, LANES).astype(BF16)
    z = _dot(a_low, wal_ref[...]) + bal_ref[...]
    g = (jnp.minimum(z, 0.0) - jnp.log1p(jnp.exp(-jnp.abs(z)))) * (1.0 / GATE_TAU)
    g_hi = g.astype(BF16)
    r1 = g - g_hi.astype(F32)
    g_mid = r1.astype(BF16)
    g_lo = (r1 - g_mid.astype(F32)).astype(BF16)
    tril3 = jnp.concatenate([tril_ref[...]] * 3, axis=1)
    b_all = _dot(tril3, jnp.concatenate([g_hi, g_mid, g_lo], axis=0)) * LOG2E

    qkv = proj(O_Q, O_GLU - O_Q)
    q_all = qkv[:, O_Q:O_K] * (GLA_DK ** -0.5)
    k_all = qkv[:, O_K:O_V]
    v_all = qkv[:, O_V:O_GLU].astype(BF16)
    ga_all = proj(O_GA, GLA_WIDTH)
    lvl = lvl_ref[...]
    lane_mod = lax.broadcasted_iota(jnp.int32, (nblk, SUBLANES, LANES), 2) & (SUBLANES - 1)

    for h in range(GLA_HEADS):
        ksl = slice(h * GLA_DK, (h + 1) * GLA_DK)
        vsl = slice(h * GLA_DV, (h + 1) * GLA_DV)
        q = q_all[:, ksl]
        k = k_all[:, ksl]
        b = b_all[:, ksl]
        v = v_all[:, vsl]

        p = jnp.zeros((ts, ts), BF16)
        for s in _level_sizes(ts):
            b3 = b.reshape(ts // (2 * s), 2 * s, GLA_DK)
            a = jnp.exp2(-jnp.abs(b3 - b3[:, s - 1:s, :])).reshape(ts, GLA_DK)
            p_l = _dot_nt((q * a).astype(BF16), (k * a).astype(BF16))
            p = jnp.where(lvl == float(np.log2(s)), p_l.astype(BF16), p)

        b3 = b.reshape(nblk, SUBLANES, GLA_DK)
        q3 = q.reshape(nblk, SUBLANES, GLA_DK)
        k3 = k.reshape(nblk, SUBLANES, GLA_DK)
        compact = jnp.zeros((nblk, SUBLANES, LANES), F32)
        for d in range(SUBLANES):
            t = q3 * jnp.exp2(b3 - b3[:, d:d + 1, :]) * k3[:, d:d + 1, :]
            compact = jnp.where(lane_mod == d, jnp.sum(t, axis=-1, keepdims=True), compact)
        compact = compact.reshape(ts, LANES)
        p = jnp.where(lvl == float(DIAG_LEVEL),
                      jnp.concatenate([compact.astype(BF16)] * (ts // LANES), axis=1), p)

        st = state_ref[h]
        b_last = b[ts - 1:ts, :]
        o = _dot(p, v) + _dot((q * jnp.exp2(b)).astype(BF16), st.astype(BF16))
        k_hat = (k * jnp.exp2(b_last - b)).astype(BF16)
        decay = jnp.transpose(jnp.broadcast_to(jnp.exp2(b_last), (SUBLANES, GLA_DK)))[:, 0:1]
        state_ref[h] = st * decay + _dot_tn(k_hat, v)

        o_ms = jnp.mean(o * o, axis=-1, keepdims=True)
        o_n = o * lax.rsqrt(o_ms + EPS) * gn_ref[...]
        ga = ga_all[:, vsl]
        mix_ref[:, vsl] = (o_n * (ga * _sigmoid(ga))).astype(BF16)

    gb_all = proj(O_GB, CONV_WIDTH)
    for gi in range(CONV_GROUPS):
        gsl = slice(gi * CONV_GROUP_WIDTH, (gi + 1) * CONV_GROUP_WIDTH)
        cg = acc_groups[gi]
        mu = jnp.mean(cg, axis=-1, keepdims=True)
        dlt = cg - mu
        var = jnp.mean(dlt * dlt, axis=-1, keepdims=True)
        y = dlt * lax.rsqrt(var + EPS) * lg_ref[:, gsl] + lb_ref[:, gsl]
        gb = gb_all[:, gsl]
        mix_ref[:, GLA_WIDTH + gi * CONV_GROUP_WIDTH:GLA_WIDTH + (gi + 1) * CONV_GROUP_WIDTH] = (
            (y * _sigmoid(y)) * (gb * _sigmoid(gb))).astype(BF16)


def _output_kernel(h_ref, mix_ref, p_ref, wout_ref, pn_ref, wg_ref, bg_ref, wp_ref, fn_ref, o_ref, *,
                   apply_final_norm):
    h = h_ref[...] + _dot(mix_ref[...], wout_ref[...])
    ms = jnp.mean(h * h, axis=-1, keepdims=True)
    n = (h * lax.rsqrt(ms + EPS) * pn_ref[...]).astype(BF16)
    gate = _sigmoid(_dot(n, wg_ref[...]) + bg_ref[...])
    h = h + gate * _dot(p_ref[...].astype(BF16), wp_ref[...])
    if apply_final_norm:
        ms2 = jnp.mean(h * h, axis=-1, keepdims=True)
        h = h * lax.rsqrt(ms2 + EPS) * fn_ref[...]
    o_ref[...] = h


def _win_relayout_kernel(src_rows_ref, wt_ref, o_ref):
    del src_rows_ref
    o_ref[...] = wt_ref[...].T.astype(BF16)


def _resident(shape):
    nd = len(shape)
    return pl.BlockSpec(shape, lambda *_: (0,) * nd, pipeline_mode=pl.Buffered(1))


def _win_relayout_call(w_in_t):
    segments = [(O_Q, 0, V_END), (O_GA, ALOW_END, GLA_WIDTH), (O_GB, CGATE_END, CONV_WIDTH),
                (O_ALOW, V_END, RELAYOUT_COLS)]
    for part in range(CONV_PARTS):
        segments.append((O_GLU + 2 * part * CONV_PART, GA_END + part * CONV_PART, CONV_PART))
        segments.append((O_GLU + (2 * part + 1) * CONV_PART, CVAL_END + part * CONV_PART, CONV_PART))
    src_rows = np.zeros((D_IN_PAD // RELAYOUT_COLS,), np.int32)
    for dst, src, width in segments:
        for off in range(0, width, RELAYOUT_COLS):
            assert (src + off) % ROW_ALIGN == 0 and src + off + RELAYOUT_COLS <= D_IN
            src_rows[(dst + off) // RELAYOUT_COLS] = (src + off) // ROW_ALIGN
    return pl.pallas_call(
        _win_relayout_kernel,
        out_shape=jax.ShapeDtypeStruct((D_MODEL, D_IN_PAD), BF16),
        grid_spec=pltpu.PrefetchScalarGridSpec(
            num_scalar_prefetch=1,
            grid=(D_IN_PAD // RELAYOUT_COLS,),
            in_specs=[pl.BlockSpec((pl.Element(RELAYOUT_COLS), pl.Element(D_MODEL)),
                                   lambda j, rows: (rows[j] * ROW_ALIGN, 0))],
            out_specs=pl.BlockSpec((D_MODEL, RELAYOUT_COLS), lambda j, rows: (0, j)),
        ),
        compiler_params=pltpu.CompilerParams(dimension_semantics=("arbitrary",)),
        name="win_relayout",
    )(jnp.asarray(src_rows), w_in_t)


def _mixer_call(h2d, seq, norm_mix, w_in_r, w_alpha_p, b_alpha, gla_norm, conv_w, conv_b, ln_g, ln_b,
                tril, lvl):
    tokens = h2d.shape[0]
    ts = SEQ_TILE
    row = lambda a: a.reshape(1, -1)
    args = (h2d, row(norm_mix), w_in_r, w_alpha_p, row(b_alpha), row(gla_norm), conv_w, row(conv_b),
            row(ln_g), row(ln_b), tril, lvl)
    in_specs = [pl.BlockSpec((ts, D_MODEL), lambda t: (t, 0))]
    in_specs += [_resident(a.shape) for a in args[1:]]
    return pl.pallas_call(
        functools.partial(_mixer_kernel, tiles_per_seq=seq // ts),
        out_shape=jax.ShapeDtypeStruct((tokens, D_MODEL), BF16),
        grid=(tokens // ts,),
        in_specs=in_specs,
        out_specs=pl.BlockSpec((ts, D_MODEL), lambda t: (t, 0)),
        scratch_shapes=[
            pltpu.VMEM((GLA_HEADS, GLA_DK, GLA_DV), F32),
            pltpu.VMEM((CONV_GROUPS, CONV_HIST + ts, CONV_GROUP_WIDTH), F32),
        ],
        compiler_params=pltpu.CompilerParams(
            dimension_semantics=("arbitrary",), vmem_limit_bytes=VMEM_LIMIT),
        name="gla_conv_mixer",
    )(*args)


def _output_call(h2d, mix2d, p2d, w_out, ple_norm, w_gate, b_gate, w_ple, final_norm, apply_final_norm):
    tokens = h2d.shape[0]
    tm = TOKEN_TILE
    row = lambda a: a.reshape(1, -1)
    tile = lambda width: pl.BlockSpec((tm, width), lambda i: (i, 0))
    weights = (w_out, row(ple_norm), w_gate, row(b_gate), w_ple, row(final_norm))
    return pl.pallas_call(
        functools.partial(_output_kernel, apply_final_norm=apply_final_norm),
        out_shape=jax.ShapeDtypeStruct((tokens, D_MODEL), F32),
        grid=(tokens // tm,),
        in_specs=[tile(D_MODEL), tile(D_MODEL), tile(PLE_DIM)] + [_resident(a.shape) for a in weights],
        out_specs=tile(D_MODEL),
        compiler_params=pltpu.CompilerParams(
            dimension_semantics=("arbitrary",), vmem_limit_bytes=VMEM_LIMIT),
        name="outproj_ple",
    )(h2d, mix2d, p2d, *weights)


def kernel(x, p, norm_mix, w_in, w_alpha, b_alpha, gla_norm, conv_w, conv_b, conv_ln_g, conv_ln_b,
           w_out, ple_norm, w_ple_gate, b_ple_gate, w_ple, final_norm):
    bsz, seq, _ = x.shape
    depth = p.shape[0]
    tokens = bsz * seq
    assert seq % SEQ_TILE == 0 and tokens % TOKEN_TILE == 0
    tril = jnp.asarray(np.tril(np.ones((SEQ_TILE, SEQ_TILE), np.float32)), BF16)
    lvl = jnp.asarray(_level_map(SEQ_TILE), BF16)

    h = x.reshape(tokens, D_MODEL)
    for i in range(depth):
        w_alpha_p = jnp.concatenate(
            [w_alpha[i], jnp.zeros((LANES - GATE_RANK, GLA_KEY_WIDTH), w_alpha.dtype)], axis=0).astype(BF16)
        conv_w_p = jnp.concatenate([conv_w[i], jnp.zeros((1, CONV_WIDTH), conv_w.dtype)], axis=0)
        w_in_t = jnp.swapaxes(w_in[i], 0, 1)
        mix = _mixer_call(h, seq, norm_mix[i], _win_relayout_call(w_in_t), w_alpha_p, b_alpha[i],
                          gla_norm[i], conv_w_p, conv_b[i], conv_ln_g[i], conv_ln_b[i], tril, lvl)
        h = _output_call(
            h, mix, p[i].reshape(tokens, PLE_DIM), w_out[i].astype(BF16), ple_norm[i],
            w_ple_gate[i].astype(BF16), b_ple_gate[i], w_ple[i].astype(BF16), final_norm,
            apply_final_norm=(i == depth - 1))
    return h.reshape(bsz, seq, D_MODEL)
```

```python
import functools
import math

import numpy as np
import jax
import jax.numpy as jnp
from jax import lax
from jax.experimental import pallas as pl
from jax.experimental.pallas import tpu as pltpu

F32 = jnp.float32
BF16 = jnp.bfloat16

D_MODEL = 2048
PLE_DIM = 256
GLA_WIDTH = 1024
CONV_WIDTH = 1024
GLA_HEADS = 4
GLA_DV = GLA_WIDTH // GLA_HEADS
GLA_DK = GLA_DV // 2
GLA_KEY_WIDTH = GLA_HEADS * GLA_DK
GATE_RANK = 16
GATE_TAU = 16.0
CONV_K = 31
CONV_GROUPS = 8
CONV_GROUP_WIDTH = CONV_WIDTH // CONV_GROUPS
EPS = 1e-6
LOG2E = math.log2(math.e)

SUBLANES = 8
LANES = 128

Q_END = GLA_KEY_WIDTH
K_END = Q_END + GLA_KEY_WIDTH
V_END = K_END + GLA_WIDTH
ALOW_END = V_END + GATE_RANK
GA_END = ALOW_END + GLA_WIDTH
CVAL_END = GA_END + CONV_WIDTH
CGATE_END = CVAL_END + CONV_WIDTH
D_IN = CGATE_END + CONV_WIDTH

RELAYOUT_COLS = 256
ROW_ALIGN = 16
CONV_PARTS = 4
CONV_PART = 1024 // CONV_PARTS

O_Q = 0
O_K = O_Q + GLA_KEY_WIDTH
O_V = O_K + GLA_KEY_WIDTH
O_GLU = O_V + GLA_WIDTH
O_GA = O_GLU + 2 * CONV_WIDTH
O_GB = O_GA + GLA_WIDTH
O_ALOW = O_GB + CONV_WIDTH
D_IN_PAD = O_ALOW + RELAYOUT_COLS

SEQ_TILE = 256
TOKEN_TILE = 512
CONV_HIST = 32
DIAG_LEVEL = 0
VMEM_LIMIT = 56 * 1024 * 1024


def _sigmoid(x):
    return 1.0 / (1.0 + jnp.exp(-x))


def _dot(a, b):
    return jnp.dot(a, b, preferred_element_type=F32)


def _dot_nt(a, b):
    return lax.dot_general(a, b, (((1,), (1,)), ((), ())), preferred_element_type=F32)


def _dot_tn(a, b):
    return lax.dot_general(a, b, (((0,), (0,)), ((), ())), preferred_element_type=F32)


def _level_sizes(ts):
    sizes = []
    s = ts // 2
    while s >= SUBLANES:
        sizes.append(s)
        s //= 2
    return tuple(sizes)


def _level_map(ts):
    i = np.arange(ts)[:, None]
    j = np.arange(ts)[None, :]
    x = np.bitwise_xor(i, j)
    top = np.floor(np.log2(np.maximum(x, 1))).astype(np.int32)
    lvl = np.where(j > i, -1, np.where(i // SUBLANES == j // SUBLANES, DIAG_LEVEL, top))
    return lvl.astype(np.int32)


def _mixer_kernel(x_ref, nm_ref, win_ref, wal_ref, bal_ref, gn_ref, cw_ref, cb_ref, lg_ref, lb_ref,
                  tril_ref, lvl_ref, mix_ref, state_ref, cbuf_ref, *, tiles_per_seq):
    ts = SEQ_TILE
    nblk = ts // SUBLANES

    @pl.when(lax.rem(pl.program_id(0), tiles_per_seq) == 0)
    def _():
        state_ref[...] = jnp.zeros_like(state_ref)
        cbuf_ref[:, 0:CONV_HIST, :] = jnp.zeros((CONV_GROUPS, CONV_HIST, CONV_GROUP_WIDTH), F32)

    x = x_ref[...]
    ms = jnp.mean(x * x, axis=-1, keepdims=True)
    u = (x * lax.rsqrt(ms + EPS) * nm_ref[...]).astype(BF16)

    def proj(lo, width):
        return _dot(u, win_ref[:, lo:lo + width])

    acc_groups = []
    for part in range(CONV_PARTS):
        glu = proj(O_GLU + 2 * part * CONV_PART, 2 * CONV_PART)
        c = glu[:, :CONV_PART] * _sigmoid(glu[:, CONV_PART:])
        for sub in range(CONV_PART // CONV_GROUP_WIDTH):
            gi = part * (CONV_PART // CONV_GROUP_WIDTH) + sub
            gsl = slice(gi * CONV_GROUP_WIDTH, (gi + 1) * CONV_GROUP_WIDTH)
            cbuf_ref[gi, CONV_HIST:CONV_HIST + ts, :] = c[:, sub * CONV_GROUP_WIDTH:(sub + 1) * CONV_GROUP_WIDTH]
            acc = jnp.broadcast_to(cb_ref[:, gsl], (ts, CONV_GROUP_WIDTH))
            for tap in range(CONV_K):
                start = CONV_HIST - (CONV_K - 1) + tap
                acc = acc + cw_ref[tap:tap + 1, gsl] * cbuf_ref[gi, start:start + ts, :]
            cbuf_ref[gi, 0:CONV_HIST, :] = cbuf_ref[gi, ts:ts + CONV_HIST, :]
            acc_groups.append(acc)

    a_low = proj(O_ALOW, LANES).astype(BF16)
    z = _dot(a_low, wal_ref[...]) + bal_ref[...]
    g = (jnp.minimum(z, 0.0) - jnp.log1p(jnp.exp(-jnp.abs(z)))) * (1.0 / GATE_TAU)
    g_hi = g.astype(BF16)
    r1 = g - g_hi.astype(F32)
    g_mid = r1.astype(BF16)
    g_lo = (r1 - g_mid.astype(F32)).astype(BF16)
    tril3 = jnp.concatenate([tril_ref[...]] * 3, axis=1)
    b_all = _dot(tril3, jnp.concatenate([g_hi, g_mid, g_lo], axis=0)) * LOG2E

    qkv = proj(O_Q, O_GLU - O_Q)
    q_all = qkv[:, O_Q:O_K] * (GLA_DK ** -0.5)
    k_all = qkv[:, O_K:O_V]
    v_all = qkv[:, O_V:O_GLU].astype(BF16)
    ga_all = proj(O_GA, GLA_WIDTH)
    lvl = lvl_ref[...]
    lane_mod = lax.broadcasted_iota(jnp.int32, (nblk, SUBLANES, LANES), 2) & (SUBLANES - 1)

    for h in range(GLA_HEADS):
        ksl = slice(h * GLA_DK, (h + 1) * GLA_DK)
        vsl = slice(h * GLA_DV, (h + 1) * GLA_DV)
        q = q_all[:, ksl]
        k = k_all[:, ksl]
        b = b_all[:, ksl]
        v = v_all[:, vsl]

        p = jnp.zeros((ts, ts), BF16)
        for s in _level_sizes(ts):
            b3 = b.reshape(ts // (2 * s), 2 * s, GLA_DK)
            a = jnp.exp2(-jnp.abs(b3 - b3[:, s - 1:s, :])).reshape(ts, GLA_DK)
            p_l = _dot_nt((q * a).astype(BF16), (k * a).astype(BF16))
            p = jnp.where(lvl == float(np.log2(s)), p_l.astype(BF16), p)

        b3 = b.reshape(nblk, SUBLANES, GLA_DK)
        q3 = q.reshape(nblk, SUBLANES, GLA_DK)
        k3 = k.reshape(nblk, SUBLANES, GLA_DK)
        compact = jnp.zeros((nblk, SUBLANES, LANES), F32)
        for d in range(SUBLANES):
            t = q3 * jnp.exp2(b3 - b3[:, d:d + 1, :]) * k3[:, d:d + 1, :]
            compact = jnp.where(lane_mod == d, jnp.sum(t, axis=-1, keepdims=True), compact)
        compact = compact.reshape(ts, LANES)
        p = jnp.where(lvl == float(DIAG_LEVEL),
                      jnp.concatenate([compact.astype(BF16)] * (ts // LANES), axis=1), p)

        st = state_ref[h]
        b_last = b[ts - 1:ts, :]
        o = _dot(p, v) + _dot((q * jnp.exp2(b)).astype(BF16), st.astype(BF16))
        k_hat = (k * jnp.exp2(b_last - b)).astype(BF16)
        decay = jnp.transpose(jnp.broadcast_to(jnp.exp2(b_last), (SUBLANES, GLA_DK)))[:, 0:1]
        state_ref[h] = st * decay + _dot_tn(k_hat, v)

        o_ms = jnp.mean(o * o, axis=-1, keepdims=True)
        o_n = o * lax.rsqrt(o_ms + EPS) * gn_ref[...]
        ga = ga_all[:, vsl]
        mix_ref[:, vsl] = (o_n * (ga * _sigmoid(ga))).astype(BF16)

    gb_all = proj(O_GB, CONV_WIDTH)
    for gi in range(CONV_GROUPS):
        gsl = slice(gi * CONV_GROUP_WIDTH, (gi + 1) * CONV_GROUP_WIDTH)
        cg = acc_groups[gi]
        mu = jnp.mean(cg, axis=-1, keepdims=True)
        dlt = cg - mu
        var = jnp.mean(dlt * dlt, axis=-1, keepdims=True)
        y = dlt * lax.rsqrt(var + EPS) * lg_ref[:, gsl] + lb_ref[:, gsl]
        gb = gb_all[:, gsl]
        mix_ref[:, GLA_WIDTH + gi * CONV_GROUP_WIDTH:GLA_WIDTH + (gi + 1) * CONV_GROUP_WIDTH] = (
            (y * _sigmoid(y)) * (gb * _sigmoid(gb))).astype(BF16)


def _output_kernel(h_ref, mix_ref, p_ref, wout_ref, pn_ref, wg_ref, bg_ref, wp_ref, fn_ref, o_ref, *,
                   apply_final_norm):
    h = h_ref[...] + _dot(mix_ref[...], wout_ref[...])
    ms = jnp.mean(h * h, axis=-1, keepdims=True)
    n = (h * lax.rsqrt(ms + EPS) * pn_ref[...]).astype(BF16)
    gate = _sigmoid(_dot(n, wg_ref[...]) + bg_ref[...])
    h = h + gate * _dot(p_ref[...].astype(BF16), wp_ref[...])
    if apply_final_norm:
        ms2 = jnp.mean(h * h, axis=-1, keepdims=True)
        h = h * lax.rsqrt(ms2 + EPS) * fn_ref[...]
    o_ref[...] = h


def _win_relayout_kernel(src_rows_ref, wt_ref, o_ref):
    del src_rows_ref
    o_ref[...] = wt_ref[...].T.astype(BF16)


SINGLE_BUFFER_BYTES = 1 << 20


def _resident(array):
    nd = array.ndim
    index_map = lambda *_: (0,) * nd
    if array.size * array.dtype.itemsize > SINGLE_BUFFER_BYTES:
        return pl.BlockSpec(array.shape, index_map, pipeline_mode=pl.Buffered(1))
    return pl.BlockSpec(array.shape, index_map)


def _win_relayout_call(w_in_t):
    segments = [(O_Q, 0, V_END), (O_GA, ALOW_END, GLA_WIDTH), (O_GB, CGATE_END, CONV_WIDTH),
                (O_ALOW, V_END, RELAYOUT_COLS)]
    for part in range(CONV_PARTS):
        segments.append((O_GLU + 2 * part * CONV_PART, GA_END + part * CONV_PART, CONV_PART))
        segments.append((O_GLU + (2 * part + 1) * CONV_PART, CVAL_END + part * CONV_PART, CONV_PART))
    src_rows = np.zeros((D_IN_PAD // RELAYOUT_COLS,), np.int32)
    for dst, src, width in segments:
        for off in range(0, width, RELAYOUT_COLS):
            assert (src + off) % ROW_ALIGN == 0 and src + off + RELAYOUT_COLS <= D_IN
            src_rows[(dst + off) // RELAYOUT_COLS] = (src + off) // ROW_ALIGN
    return pl.pallas_call(
        _win_relayout_kernel,
        out_shape=jax.ShapeDtypeStruct((D_MODEL, D_IN_PAD), BF16),
        grid_spec=pltpu.PrefetchScalarGridSpec(
            num_scalar_prefetch=1,
            grid=(D_IN_PAD // RELAYOUT_COLS,),
            in_specs=[pl.BlockSpec((pl.Element(RELAYOUT_COLS), pl.Element(D_MODEL)),
                                   lambda j, rows: (rows[j] * ROW_ALIGN, 0))],
            out_specs=pl.BlockSpec((D_MODEL, RELAYOUT_COLS), lambda j, rows: (0, j)),
        ),
        compiler_params=pltpu.CompilerParams(dimension_semantics=("arbitrary",)),
        name="win_relayout",
    )(jnp.asarray(src_rows), w_in_t)


def _mixer_call(h2d, seq, norm_mix, w_in_r, w_alpha_p, b_alpha, gla_norm, conv_w, conv_b, ln_g, ln_b,
                tril, lvl):
    tokens = h2d.shape[0]
    ts = SEQ_TILE
    row = lambda a: a.reshape(1, -1)
    args = (h2d, row(norm_mix), w_in_r, w_alpha_p, row(b_alpha), row(gla_norm), conv_w, row(conv_b),
            row(ln_g), row(ln_b), tril, lvl)
    in_specs = [pl.BlockSpec((ts, D_MODEL), lambda t: (t, 0))]
    in_specs += [_resident(a) for a in args[1:]]
    return pl.pallas_call(
        functools.partial(_mixer_kernel, tiles_per_seq=seq // ts),
        out_shape=jax.ShapeDtypeStruct((tokens, D_MODEL), BF16),
        grid=(tokens // ts,),
        in_specs=in_specs,
        out_specs=pl.BlockSpec((ts, D_MODEL), lambda t: (t, 0)),
        scratch_shapes=[
            pltpu.VMEM((GLA_HEADS, GLA_DK, GLA_DV), F32),
            pltpu.VMEM((CONV_GROUPS, CONV_HIST + ts, CONV_GROUP_WIDTH), F32),
        ],
        compiler_params=pltpu.CompilerParams(
            dimension_semantics=("arbitrary",), vmem_limit_bytes=VMEM_LIMIT),
        name="gla_conv_mixer",
    )(*args)


def _output_call(h2d, mix2d, p2d, w_out, ple_norm, w_gate, b_gate, w_ple, final_norm, apply_final_norm):
    tokens = h2d.shape[0]
    tm = TOKEN_TILE
    row = lambda a: a.reshape(1, -1)
    tile = lambda width: pl.BlockSpec((tm, width), lambda i: (i, 0))
    weights = (w_out, row(ple_norm), w_gate, row(b_gate), w_ple, row(final_norm))
    return pl.pallas_call(
        functools.partial(_output_kernel, apply_final_norm=apply_final_norm),
        out_shape=jax.ShapeDtypeStruct((tokens, D_MODEL), F32),
        grid=(tokens // tm,),
        in_specs=[tile(D_MODEL), tile(D_MODEL), tile(PLE_DIM)] + [_resident(a) for a in weights],
        out_specs=tile(D_MODEL),
        compiler_params=pltpu.CompilerParams(
            dimension_semantics=("arbitrary",), vmem_limit_bytes=VMEM_LIMIT),
        name="outproj_ple",
    )(h2d, mix2d, p2d, *weights)


def kernel(x, p, norm_mix, w_in, w_alpha, b_alpha, gla_norm, conv_w, conv_b, conv_ln_g, conv_ln_b,
           w_out, ple_norm, w_ple_gate, b_ple_gate, w_ple, final_norm):
    bsz, seq, _ = x.shape
    depth = p.shape[0]
    tokens = bsz * seq
    assert seq % SEQ_TILE == 0 and tokens % TOKEN_TILE == 0
    tril = jnp.asarray(np.tril(np.ones((SEQ_TILE, SEQ_TILE), np.float32)), BF16)
    lvl = jnp.asarray(_level_map(SEQ_TILE), BF16)

    h = x.reshape(tokens, D_MODEL)
    for i in range(depth):
        w_alpha_p = jnp.concatenate(
            [w_alpha[i], jnp.zeros((LANES - GATE_RANK, GLA_KEY_WIDTH), w_alpha.dtype)], axis=0).astype(BF16)
        conv_w_p = jnp.concatenate([conv_w[i], jnp.zeros((1, CONV_WIDTH), conv_w.dtype)], axis=0)
        w_in_t = jnp.swapaxes(w_in[i], 0, 1)
        mix = _mixer_call(h, seq, norm_mix[i], _win_relayout_call(w_in_t), w_alpha_p, b_alpha[i],
                          gla_norm[i], conv_w_p, conv_b[i], conv_ln_g[i], conv_ln_b[i], tril, lvl)
        h = _output_call(
            h, mix, p[i].reshape(tokens, PLE_DIM), w_out[i].astype(BF16), ple_norm[i],
            w_ple_gate[i].astype(BF16), b_ple_gate[i], w_ple[i].astype(BF16), final_norm,
            apply_final_norm=(i == depth - 1))
    return h.reshape(bsz, seq, D_MODEL)
```

```python
import functools
import math

import numpy as np
import jax
import jax.numpy as jnp
from jax import lax
from jax.experimental import pallas as pl
from jax.experimental.pallas import tpu as pltpu

F32 = jnp.float32
BF16 = jnp.bfloat16

D_MODEL = 2048
PLE_DIM = 256
GLA_WIDTH = 1024
CONV_WIDTH = 1024
GLA_HEADS = 4
GLA_DV = GLA_WIDTH // GLA_HEADS
GLA_DK = GLA_DV // 2
GLA_KEY_WIDTH = GLA_HEADS * GLA_DK
GATE_RANK = 16
GATE_TAU = 16.0
CONV_K = 31
CONV_GROUPS = 8
CONV_GROUP_WIDTH = CONV_WIDTH // CONV_GROUPS
EPS = 1e-6
LOG2E = math.log2(math.e)

SUBLANES = 8
LANES = 128

Q_END = GLA_KEY_WIDTH
K_END = Q_END + GLA_KEY_WIDTH
V_END = K_END + GLA_WIDTH
ALOW_END = V_END + GATE_RANK
GA_END = ALOW_END + GLA_WIDTH
CVAL_END = GA_END + CONV_WIDTH
CGATE_END = CVAL_END + CONV_WIDTH
D_IN = CGATE_END + CONV_WIDTH

RELAYOUT_COLS = 256
ROW_ALIGN = 16
CONV_PARTS = 4
CONV_PART = 1024 // CONV_PARTS

O_Q = 0
O_K = O_Q + GLA_KEY_WIDTH
O_V = O_K + GLA_KEY_WIDTH
O_GLU = O_V + GLA_WIDTH
O_GA = O_GLU + 2 * CONV_WIDTH
O_GB = O_GA + GLA_WIDTH
O_ALOW = O_GB + CONV_WIDTH
D_IN_PAD = O_ALOW + RELAYOUT_COLS

SEQ_TILE = 256
TOKEN_TILE = 512
CONV_HIST = 32
DIAG_LEVEL = 0
VMEM_LIMIT = 56 * 1024 * 1024


def _sigmoid(x):
    return 1.0 / (1.0 + jnp.exp(-x))


def _dot(a, b):
    return jnp.dot(a, b, preferred_element_type=F32)


def _dot_nt(a, b):
    return lax.dot_general(a, b, (((1,), (1,)), ((), ())), preferred_element_type=F32)


def _dot_tn(a, b):
    return lax.dot_general(a, b, (((0,), (0,)), ((), ())), preferred_element_type=F32)


def _level_sizes(ts):
    sizes = []
    s = ts // 2
    while s >= SUBLANES:
        sizes.append(s)
        s //= 2
    return tuple(sizes)


def _level_map(ts):
    i = np.arange(ts)[:, None]
    j = np.arange(ts)[None, :]
    x = np.bitwise_xor(i, j)
    top = np.floor(np.log2(np.maximum(x, 1))).astype(np.int32)
    lvl = np.where(j > i, -1, np.where(i // SUBLANES == j // SUBLANES, DIAG_LEVEL, top))
    return lvl.astype(np.int32)


def _mixer_kernel(x_ref, nm_ref, win_ref, wal_ref, bal_ref, gn_ref, cw_ref, cb_ref, lg_ref, lb_ref,
                  tril_ref, lvl_ref, wout_ref, wgate_ref, mix_ref, wout16_ref, wgate16_ref, state_ref,
                  cbuf_ref, *, tiles_per_seq):
    ts = SEQ_TILE
    nblk = ts // SUBLANES
    wout16_ref[...] = wout_ref[...].astype(BF16)
    wgate16_ref[...] = wgate_ref[...].astype(BF16)

    @pl.when(lax.rem(pl.program_id(0), tiles_per_seq) == 0)
    def _():
        state_ref[...] = jnp.zeros_like(state_ref)
        cbuf_ref[:, 0:CONV_HIST, :] = jnp.zeros((CONV_GROUPS, CONV_HIST, CONV_GROUP_WIDTH), F32)

    x = x_ref[...]
    ms = jnp.mean(x * x, axis=-1, keepdims=True)
    u = (x * lax.rsqrt(ms + EPS) * nm_ref[...]).astype(BF16)

    def proj(lo, width):
        return _dot(u, win_ref[:, lo:lo + width])

    acc_groups = []
    for part in range(CONV_PARTS):
        glu = proj(O_GLU + 2 * part * CONV_PART, 2 * CONV_PART)
        c = glu[:, :CONV_PART] * _sigmoid(glu[:, CONV_PART:])
        for sub in range(CONV_PART // CONV_GROUP_WIDTH):
            gi = part * (CONV_PART // CONV_GROUP_WIDTH) + sub
            gsl = slice(gi * CONV_GROUP_WIDTH, (gi + 1) * CONV_GROUP_WIDTH)
            cbuf_ref[gi, CONV_HIST:CONV_HIST + ts, :] = c[:, sub * CONV_GROUP_WIDTH:(sub + 1) * CONV_GROUP_WIDTH]
            acc = jnp.broadcast_to(cb_ref[:, gsl], (ts, CONV_GROUP_WIDTH))
            for tap in range(CONV_K):
                start = CONV_HIST - (CONV_K - 1) + tap
                acc = acc + cw_ref[tap:tap + 1, gsl] * cbuf_ref[gi, start:start + ts, :]
            cbuf_ref[gi, 0:CONV_HIST, :] = cbuf_ref[gi, ts:ts + CONV_HIST, :]
            acc_groups.append(acc)

    a_low = proj(O_ALOW, LANES).astype(BF16)
    z = _dot(a_low, wal_ref[...]) + bal_ref[...]
    g = (jnp.minimum(z, 0.0) - jnp.log1p(jnp.exp(-jnp.abs(z)))) * (1.0 / GATE_TAU)
    g_hi = g.astype(BF16)
    r1 = g - g_hi.astype(F32)
    g_mid = r1.astype(BF16)
    g_lo = (r1 - g_mid.astype(F32)).astype(BF16)
    tril3 = jnp.concatenate([tril_ref[...]] * 3, axis=1)
    b_all = _dot(tril3, jnp.concatenate([g_hi, g_mid, g_lo], axis=0)) * LOG2E

    qkv = proj(O_Q, O_GLU - O_Q)
    q_all = qkv[:, O_Q:O_K] * (GLA_DK ** -0.5)
    k_all = qkv[:, O_K:O_V]
    v_all = qkv[:, O_V:O_GLU].astype(BF16)
    ga_all = proj(O_GA, GLA_WIDTH)
    lvl = lvl_ref[...]
    lane_mod = lax.broadcasted_iota(jnp.int32, (nblk, SUBLANES, LANES), 2) & (SUBLANES - 1)

    for h in range(GLA_HEADS):
        ksl = slice(h * GLA_DK, (h + 1) * GLA_DK)
        vsl = slice(h * GLA_DV, (h + 1) * GLA_DV)
        q = q_all[:, ksl]
        k = k_all[:, ksl]
        b = b_all[:, ksl]
        v = v_all[:, vsl]

        p = jnp.zeros((ts, ts), BF16)
        for s in _level_sizes(ts):
            b3 = b.reshape(ts // (2 * s), 2 * s, GLA_DK)
            a = jnp.exp2(-jnp.abs(b3 - b3[:, s - 1:s, :])).reshape(ts, GLA_DK)
            p_l = _dot_nt((q * a).astype(BF16), (k * a).astype(BF16))
            p = jnp.where(lvl == float(np.log2(s)), p_l.astype(BF16), p)

        b3 = b.reshape(nblk, SUBLANES, GLA_DK)
        q3 = q.reshape(nblk, SUBLANES, GLA_DK)
        k3 = k.reshape(nblk, SUBLANES, GLA_DK)
        compact = jnp.zeros((nblk, SUBLANES, LANES), F32)
        for d in range(SUBLANES):
            t = q3 * jnp.exp2(b3 - b3[:, d:d + 1, :]) * k3[:, d:d + 1, :]
            compact = jnp.where(lane_mod == d, jnp.sum(t, axis=-1, keepdims=True), compact)
        compact = compact.reshape(ts, LANES)
        p = jnp.where(lvl == float(DIAG_LEVEL),
                      jnp.concatenate([compact.astype(BF16)] * (ts // LANES), axis=1), p)

        st = state_ref[h]
        b_last = b[ts - 1:ts, :]
        o = _dot(p, v) + _dot((q * jnp.exp2(b)).astype(BF16), st.astype(BF16))
        k_hat = (k * jnp.exp2(b_last - b)).astype(BF16)
        decay = jnp.transpose(jnp.broadcast_to(jnp.exp2(b_last), (SUBLANES, GLA_DK)))[:, 0:1]
        state_ref[h] = st * decay + _dot_tn(k_hat, v)

        o_ms = jnp.mean(o * o, axis=-1, keepdims=True)
        o_n = o * lax.rsqrt(o_ms + EPS) * gn_ref[...]
        ga = ga_all[:, vsl]
        mix_ref[:, vsl] = (o_n * (ga * _sigmoid(ga))).astype(BF16)

    gb_all = proj(O_GB, CONV_WIDTH)
    for gi in range(CONV_GROUPS):
        gsl = slice(gi * CONV_GROUP_WIDTH, (gi + 1) * CONV_GROUP_WIDTH)
        cg = acc_groups[gi]
        mu = jnp.mean(cg, axis=-1, keepdims=True)
        dlt = cg - mu
        var = jnp.mean(dlt * dlt, axis=-1, keepdims=True)
        y = dlt * lax.rsqrt(var + EPS) * lg_ref[:, gsl] + lb_ref[:, gsl]
        gb = gb_all[:, gsl]
        mix_ref[:, GLA_WIDTH + gi * CONV_GROUP_WIDTH:GLA_WIDTH + (gi + 1) * CONV_GROUP_WIDTH] = (
            (y * _sigmoid(y)) * (gb * _sigmoid(gb))).astype(BF16)


def _output_kernel(h_ref, mix_ref, p_ref, wout_ref, pn_ref, wg_ref, bg_ref, wp_ref, fn_ref, o_ref, *,
                   apply_final_norm):
    h = h_ref[...] + _dot(mix_ref[...], wout_ref[...])
    ms = jnp.mean(h * h, axis=-1, keepdims=True)
    n = (h * lax.rsqrt(ms + EPS) * pn_ref[...]).astype(BF16)
    gate = _sigmoid(_dot(n, wg_ref[...]) + bg_ref[...])
    h = h + gate * _dot(p_ref[...].astype(BF16), wp_ref[...])
    if apply_final_norm:
        ms2 = jnp.mean(h * h, axis=-1, keepdims=True)
        h = h * lax.rsqrt(ms2 + EPS) * fn_ref[...]
    o_ref[...] = h


def _win_relayout_kernel(src_rows_ref, wt_ref, o_ref):
    del src_rows_ref
    o_ref[...] = wt_ref[...].T.astype(BF16)


def _resident(shape):
    nd = len(shape)
    return pl.BlockSpec(shape, lambda *_: (0,) * nd, pipeline_mode=pl.Buffered(1))


def _win_relayout_call(w_in_t):
    segments = [(O_Q, 0, V_END), (O_GA, ALOW_END, GLA_WIDTH), (O_GB, CGATE_END, CONV_WIDTH),
                (O_ALOW, V_END, RELAYOUT_COLS)]
    for part in range(CONV_PARTS):
        segments.append((O_GLU + 2 * part * CONV_PART, GA_END + part * CONV_PART, CONV_PART))
        segments.append((O_GLU + (2 * part + 1) * CONV_PART, CVAL_END + part * CONV_PART, CONV_PART))
    src_rows = np.zeros((D_IN_PAD // RELAYOUT_COLS,), np.int32)
    for dst, src, width in segments:
        for off in range(0, width, RELAYOUT_COLS):
            assert (src + off) % ROW_ALIGN == 0 and src + off + RELAYOUT_COLS <= D_IN
            src_rows[(dst + off) // RELAYOUT_COLS] = (src + off) // ROW_ALIGN
    return pl.pallas_call(
        _win_relayout_kernel,
        out_shape=jax.ShapeDtypeStruct((D_MODEL, D_IN_PAD), BF16),
        grid_spec=pltpu.PrefetchScalarGridSpec(
            num_scalar_prefetch=1,
            grid=(D_IN_PAD // RELAYOUT_COLS,),
            in_specs=[pl.BlockSpec((pl.Element(RELAYOUT_COLS), pl.Element(D_MODEL)),
                                   lambda j, rows: (rows[j] * ROW_ALIGN, 0))],
            out_specs=pl.BlockSpec((D_MODEL, RELAYOUT_COLS), lambda j, rows: (0, j)),
        ),
        compiler_params=pltpu.CompilerParams(dimension_semantics=("arbitrary",)),
        name="win_relayout",
    )(jnp.asarray(src_rows), w_in_t)


def _mixer_call(h2d, seq, norm_mix, w_in_r, w_alpha_p, b_alpha, gla_norm, conv_w, conv_b, ln_g, ln_b,
                tril, lvl, w_out, w_gate):
    tokens = h2d.shape[0]
    ts = SEQ_TILE
    steps = tokens // ts
    cast_rows = D_MODEL // steps
    assert cast_rows * steps == D_MODEL and cast_rows % (2 * SUBLANES) == 0
    row = lambda a: a.reshape(1, -1)
    args = (h2d, row(norm_mix), w_in_r, w_alpha_p, row(b_alpha), row(gla_norm), conv_w, row(conv_b),
            row(ln_g), row(ln_b), tril, lvl)
    in_specs = [pl.BlockSpec((ts, D_MODEL), lambda t: (t, 0))]
    in_specs += [_resident(a.shape) for a in args[1:]]
    weight_slice = pl.BlockSpec((cast_rows, D_MODEL), lambda t: (t, 0))
    in_specs += [weight_slice, weight_slice]
    return pl.pallas_call(
        functools.partial(_mixer_kernel, tiles_per_seq=seq // ts),
        out_shape=(jax.ShapeDtypeStruct((tokens, D_MODEL), BF16),
                   jax.ShapeDtypeStruct((D_MODEL, D_MODEL), BF16),
                   jax.ShapeDtypeStruct((D_MODEL, D_MODEL), BF16)),
        grid=(steps,),
        in_specs=in_specs,
        out_specs=(pl.BlockSpec((ts, D_MODEL), lambda t: (t, 0)), weight_slice, weight_slice),
        scratch_shapes=[
            pltpu.VMEM((GLA_HEADS, GLA_DK, GLA_DV), F32),
            pltpu.VMEM((CONV_GROUPS, CONV_HIST + ts, CONV_GROUP_WIDTH), F32),
        ],
        compiler_params=pltpu.CompilerParams(
            dimension_semantics=("arbitrary",), vmem_limit_bytes=VMEM_LIMIT),
        name="gla_conv_mixer",
    )(*args, w_out, w_gate)


def _output_call(h2d, mix2d, p2d, w_out, ple_norm, w_gate, b_gate, w_ple, final_norm, apply_final_norm):
    tokens = h2d.shape[0]
    tm = TOKEN_TILE
    row = lambda a: a.reshape(1, -1)
    tile = lambda width: pl.BlockSpec((tm, width), lambda i: (i, 0))
    weights = (w_out, row(ple_norm), w_gate, row(b_gate), w_ple, row(final_norm))
    return pl.pallas_call(
        functools.partial(_output_kernel, apply_final_norm=apply_final_norm),
        out_shape=jax.ShapeDtypeStruct((tokens, D_MODEL), F32),
        grid=(tokens // tm,),
        in_specs=[tile(D_MODEL), tile(D_MODEL), tile(PLE_DIM)] + [_resident(a.shape) for a in weights],
        out_specs=tile(D_MODEL),
        compiler_params=pltpu.CompilerParams(
            dimension_semantics=("arbitrary",), vmem_limit_bytes=VMEM_LIMIT),
        name="outproj_ple",
    )(h2d, mix2d, p2d, *weights)


def kernel(x, p, norm_mix, w_in, w_alpha, b_alpha, gla_norm, conv_w, conv_b, conv_ln_g, conv_ln_b,
           w_out, ple_norm, w_ple_gate, b_ple_gate, w_ple, final_norm):
    bsz, seq, _ = x.shape
    depth = p.shape[0]
    tokens = bsz * seq
    assert seq % SEQ_TILE == 0 and tokens % TOKEN_TILE == 0
    tril = jnp.asarray(np.tril(np.ones((SEQ_TILE, SEQ_TILE), np.float32)), BF16)
    lvl = jnp.asarray(_level_map(SEQ_TILE), BF16)

    h = x.reshape(tokens, D_MODEL)
    for i in range(depth):
        w_alpha_p = jnp.concatenate(
            [w_alpha[i], jnp.zeros((LANES - GATE_RANK, GLA_KEY_WIDTH), w_alpha.dtype)], axis=0).astype(BF16)
        conv_w_p = jnp.concatenate([conv_w[i], jnp.zeros((1, CONV_WIDTH), conv_w.dtype)], axis=0)
        w_in_t = jnp.swapaxes(w_in[i], 0, 1)
        mix, w_out16, w_gate16 = _mixer_call(
            h, seq, norm_mix[i], _win_relayout_call(w_in_t), w_alpha_p, b_alpha[i], gla_norm[i], conv_w_p,
            conv_b[i], conv_ln_g[i], conv_ln_b[i], tril, lvl, w_out[i], w_ple_gate[i])
        h = _output_call(
            h, mix, p[i].reshape(tokens, PLE_DIM), w_out16, ple_norm[i], w_gate16, b_ple_gate[i],
            w_ple[i].astype(BF16), final_norm, apply_final_norm=(i == depth - 1))
    return h.reshape(bsz, seq, D_MODEL)
```

```python
import functools
import math

import numpy as np
import jax
import jax.numpy as jnp
from jax import lax
from jax.experimental import pallas as pl
from jax.experimental.pallas import tpu as pltpu

F32 = jnp.float32
BF16 = jnp.bfloat16

D_MODEL = 2048
PLE_DIM = 256
GLA_WIDTH = 1024
CONV_WIDTH = 1024
GLA_HEADS = 4
GLA_DV = GLA_WIDTH // GLA_HEADS
GLA_DK = GLA_DV // 2
GLA_KEY_WIDTH = GLA_HEADS * GLA_DK
GATE_RANK = 16
GATE_TAU = 16.0
CONV_K = 31
CONV_GROUPS = 8
CONV_GROUP_WIDTH = CONV_WIDTH // CONV_GROUPS
EPS = 1e-6
LOG2E = math.log2(math.e)

SUBLANES = 8
LANES = 128

Q_END = GLA_KEY_WIDTH
K_END = Q_END + GLA_KEY_WIDTH
V_END = K_END + GLA_WIDTH
ALOW_END = V_END + GATE_RANK
GA_END = ALOW_END + GLA_WIDTH
CVAL_END = GA_END + CONV_WIDTH
CGATE_END = CVAL_END + CONV_WIDTH
D_IN = CGATE_END + CONV_WIDTH

RELAYOUT_COLS = 256
ROW_ALIGN = 16
CONV_PARTS = 4
CONV_PART = CONV_WIDTH // CONV_PARTS
assert CONV_PART % RELAYOUT_COLS == 0 and CONV_PART % CONV_GROUP_WIDTH == 0

O_Q = 0
O_K = O_Q + GLA_KEY_WIDTH
O_V = O_K + GLA_KEY_WIDTH
O_GLU = O_V + GLA_WIDTH
O_GA = O_GLU + 2 * CONV_WIDTH
O_GB = O_GA + GLA_WIDTH
O_ALOW = O_GB + CONV_WIDTH
D_IN_PAD = O_ALOW + RELAYOUT_COLS

SEQ_TILE = 256
TOKEN_TILE = 512
CONV_HIST = 32
DIAG_LEVEL = 0
VMEM_LIMIT = 56 * 1024 * 1024


def _sigmoid(x):
    return 1.0 / (1.0 + jnp.exp(-x))


def _dot(a, b):
    return jnp.dot(a, b, preferred_element_type=F32)


def _dot_nt(a, b):
    return lax.dot_general(a, b, (((1,), (1,)), ((), ())), preferred_element_type=F32)


def _dot_tn(a, b):
    return lax.dot_general(a, b, (((0,), (0,)), ((), ())), preferred_element_type=F32)


def _level_sizes(ts):
    sizes = []
    s = ts // 2
    while s >= SUBLANES:
        sizes.append(s)
        s //= 2
    return tuple(sizes)


def _level_map(ts):
    i = np.arange(ts)[:, None]
    j = np.arange(ts)[None, :]
    x = np.bitwise_xor(i, j)
    top = np.floor(np.log2(np.maximum(x, 1))).astype(np.int32)
    lvl = np.where(j > i, -1, np.where(i // SUBLANES == j // SUBLANES, DIAG_LEVEL, top))
    return lvl.astype(np.int32)


def _mixer_kernel(x_ref, nm_ref, win_ref, wal_ref, bal_ref, gn_ref, cw_ref, cb_ref, lg_ref, lb_ref,
                  tril_ref, lvl_ref, wout_ref, wgate_ref, mix_ref, wout16_ref, wgate16_ref, state_ref,
                  cbuf_ref, *, tiles_per_seq):
    ts = SEQ_TILE
    nblk = ts // SUBLANES
    wout16_ref[...] = wout_ref[...].astype(BF16)
    wgate16_ref[...] = wgate_ref[...].astype(BF16)

    @pl.when(lax.rem(pl.program_id(0), tiles_per_seq) == 0)
    def _():
        state_ref[...] = jnp.zeros_like(state_ref)
        cbuf_ref[:, 0:CONV_HIST, :] = jnp.zeros((CONV_GROUPS, CONV_HIST, CONV_GROUP_WIDTH), F32)

    x = x_ref[...]
    ms = jnp.mean(x * x, axis=-1, keepdims=True)
    u = (x * lax.rsqrt(ms + EPS) * nm_ref[...]).astype(BF16)

    def proj(lo, width):
        return _dot(u, win_ref[:, lo:lo + width])

    acc_groups = []
    for part in range(CONV_PARTS):
        glu = proj(O_GLU + 2 * part * CONV_PART, 2 * CONV_PART)
        c = glu[:, :CONV_PART] * _sigmoid(glu[:, CONV_PART:])
        for sub in range(CONV_PART // CONV_GROUP_WIDTH):
            gi = part * (CONV_PART // CONV_GROUP_WIDTH) + sub
            gsl = slice(gi * CONV_GROUP_WIDTH, (gi + 1) * CONV_GROUP_WIDTH)
            cbuf_ref[gi, CONV_HIST:CONV_HIST + ts, :] = c[:, sub * CONV_GROUP_WIDTH:(sub + 1) * CONV_GROUP_WIDTH]
            acc = jnp.broadcast_to(cb_ref[:, gsl], (ts, CONV_GROUP_WIDTH))
            for tap in range(CONV_K):
                start = CONV_HIST - (CONV_K - 1) + tap
                acc = acc + cw_ref[tap:tap + 1, gsl] * cbuf_ref[gi, start:start + ts, :]
            cbuf_ref[gi, 0:CONV_HIST, :] = cbuf_ref[gi, ts:ts + CONV_HIST, :]
            acc_groups.append(acc)

    a_low = proj(O_ALOW, LANES).astype(BF16)
    z = _dot(a_low, wal_ref[...]) + bal_ref[...]
    g = (jnp.minimum(z, 0.0) - jnp.log1p(jnp.exp(-jnp.abs(z)))) * (1.0 / GATE_TAU)
    g_hi = g.astype(BF16)
    r1 = g - g_hi.astype(F32)
    g_mid = r1.astype(BF16)
    g_lo = (r1 - g_mid.astype(F32)).astype(BF16)
    tril3 = jnp.concatenate([tril_ref[...]] * 3, axis=1)
    b_all = _dot(tril3, jnp.concatenate([g_hi, g_mid, g_lo], axis=0)) * LOG2E

    qkv = proj(O_Q, O_GLU - O_Q)
    q_all = qkv[:, O_Q:O_K] * (GLA_DK ** -0.5)
    k_all = qkv[:, O_K:O_V]
    v_all = qkv[:, O_V:O_GLU].astype(BF16)
    ga_all = proj(O_GA, GLA_WIDTH)
    lvl = lvl_ref[...]
    lane_mod = lax.broadcasted_iota(jnp.int32, (nblk, SUBLANES, LANES), 2) & (SUBLANES - 1)

    for h in range(GLA_HEADS):
        ksl = slice(h * GLA_DK, (h + 1) * GLA_DK)
        vsl = slice(h * GLA_DV, (h + 1) * GLA_DV)
        q = q_all[:, ksl]
        k = k_all[:, ksl]
        b = b_all[:, ksl]
        v = v_all[:, vsl]

        p = jnp.zeros((ts, ts), BF16)
        for s in _level_sizes(ts):
            b3 = b.reshape(ts // (2 * s), 2 * s, GLA_DK)
            a = jnp.exp2(-jnp.abs(b3 - b3[:, s - 1:s, :])).reshape(ts, GLA_DK)
            p_l = _dot_nt((q * a).astype(BF16), (k * a).astype(BF16))
            p = jnp.where(lvl == float(np.log2(s)), p_l.astype(BF16), p)

        b3 = b.reshape(nblk, SUBLANES, GLA_DK)
        q3 = q.reshape(nblk, SUBLANES, GLA_DK)
        k3 = k.reshape(nblk, SUBLANES, GLA_DK)
        compact = jnp.zeros((nblk, SUBLANES, LANES), F32)
        for d in range(SUBLANES):
            t = q3 * jnp.exp2(b3 - b3[:, d:d + 1, :]) * k3[:, d:d + 1, :]
            compact = jnp.where(lane_mod == d, jnp.sum(t, axis=-1, keepdims=True), compact)
        compact = compact.reshape(ts, LANES)
        p = jnp.where(lvl == float(DIAG_LEVEL),
                      jnp.concatenate([compact.astype(BF16)] * (ts // LANES), axis=1), p)

        st = state_ref[h]
        b_last = b[ts - 1:ts, :]
        o = _dot(p, v) + _dot((q * jnp.exp2(b)).astype(BF16), st.astype(BF16))
        k_hat = (k * jnp.exp2(b_last - b)).astype(BF16)
        decay = jnp.transpose(jnp.broadcast_to(jnp.exp2(b_last), (SUBLANES, GLA_DK)))[:, 0:1]
        state_ref[h] = st * decay + _dot_tn(k_hat, v)

        o_ms = jnp.mean(o * o, axis=-1, keepdims=True)
        o_n = o * lax.rsqrt(o_ms + EPS) * gn_ref[...]
        ga = ga_all[:, vsl]
        mix_ref[:, vsl] = (o_n * (ga * _sigmoid(ga))).astype(BF16)

    gb_all = proj(O_GB, CONV_WIDTH)
    for gi in range(CONV_GROUPS):
        gsl = slice(gi * CONV_GROUP_WIDTH, (gi + 1) * CONV_GROUP_WIDTH)
        cg = acc_groups[gi]
        mu = jnp.mean(cg, axis=-1, keepdims=True)
        dlt = cg - mu
        var = jnp.mean(dlt * dlt, axis=-1, keepdims=True)
        y = dlt * lax.rsqrt(var + EPS) * lg_ref[:, gsl] + lb_ref[:, gsl]
        gb = gb_all[:, gsl]
        mix_ref[:, GLA_WIDTH + gi * CONV_GROUP_WIDTH:GLA_WIDTH + (gi + 1) * CONV_GROUP_WIDTH] = (
            (y * _sigmoid(y)) * (gb * _sigmoid(gb))).astype(BF16)


def _output_kernel(h_ref, mix_ref, p_ref, wout_ref, pn_ref, wg_ref, bg_ref, wp_ref, fn_ref, o_ref, *,
                   apply_final_norm):
    h = h_ref[...] + _dot(mix_ref[...], wout_ref[...])
    ms = jnp.mean(h * h, axis=-1, keepdims=True)
    n = (h * lax.rsqrt(ms + EPS) * pn_ref[...]).astype(BF16)
    gate = _sigmoid(_dot(n, wg_ref[...]) + bg_ref[...])
    h = h + gate * _dot(p_ref[...].astype(BF16), wp_ref[...])
    if apply_final_norm:
        ms2 = jnp.mean(h * h, axis=-1, keepdims=True)
        h = h * lax.rsqrt(ms2 + EPS) * fn_ref[...]
    o_ref[...] = h


def _win_relayout_kernel(src_rows_ref, wt_ref, o_ref):
    del src_rows_ref
    o_ref[...] = wt_ref[...].T.astype(BF16)


def _resident(shape):
    nd = len(shape)
    return pl.BlockSpec(shape, lambda *_: (0,) * nd, pipeline_mode=pl.Buffered(1))


def _win_relayout_call(w_in_t):
    segments = [(O_Q, 0, V_END), (O_GA, ALOW_END, GLA_WIDTH), (O_GB, CGATE_END, CONV_WIDTH),
                (O_ALOW, V_END, RELAYOUT_COLS)]
    for part in range(CONV_PARTS):
        segments.append((O_GLU + 2 * part * CONV_PART, GA_END + part * CONV_PART, CONV_PART))
        segments.append((O_GLU + (2 * part + 1) * CONV_PART, CVAL_END + part * CONV_PART, CONV_PART))
    src_rows = np.zeros((D_IN_PAD // RELAYOUT_COLS,), np.int32)
    for dst, src, width in segments:
        for off in range(0, width, RELAYOUT_COLS):
            assert (src + off) % ROW_ALIGN == 0 and src + off + RELAYOUT_COLS <= D_IN
            src_rows[(dst + off) // RELAYOUT_COLS] = (src + off) // ROW_ALIGN
    return pl.pallas_call(
        _win_relayout_kernel,
        out_shape=jax.ShapeDtypeStruct((D_MODEL, D_IN_PAD), BF16),
        grid_spec=pltpu.PrefetchScalarGridSpec(
            num_scalar_prefetch=1,
            grid=(D_IN_PAD // RELAYOUT_COLS,),
            in_specs=[pl.BlockSpec((pl.Element(RELAYOUT_COLS), pl.Element(D_MODEL)),
                                   lambda j, rows: (rows[j] * ROW_ALIGN, 0))],
            out_specs=pl.BlockSpec((D_MODEL, RELAYOUT_COLS), lambda j, rows: (0, j)),
        ),
        compiler_params=pltpu.CompilerParams(dimension_semantics=("arbitrary",)),
        name="win_relayout",
    )(jnp.asarray(src_rows), w_in_t)


def _mixer_call(h2d, seq, norm_mix, w_in_r, w_alpha_p, b_alpha, gla_norm, conv_w, conv_b, ln_g, ln_b,
                tril, lvl, w_out, w_gate):
    tokens = h2d.shape[0]
    ts = SEQ_TILE
    steps = tokens // ts
    cast_rows = D_MODEL // steps
    assert cast_rows * steps == D_MODEL and cast_rows % (2 * SUBLANES) == 0
    row = lambda a: a.reshape(1, -1)
    args = (h2d, row(norm_mix), w_in_r, w_alpha_p, row(b_alpha), row(gla_norm), conv_w, row(conv_b),
            row(ln_g), row(ln_b), tril, lvl)
    in_specs = [pl.BlockSpec((ts, D_MODEL), lambda t: (t, 0))]
    in_specs += [_resident(a.shape) for a in args[1:]]
    weight_slice = pl.BlockSpec((cast_rows, D_MODEL), lambda t: (t, 0))
    in_specs += [weight_slice, weight_slice]
    return pl.pallas_call(
        functools.partial(_mixer_kernel, tiles_per_seq=seq // ts),
        out_shape=(jax.ShapeDtypeStruct((tokens, D_MODEL), BF16),
                   jax.ShapeDtypeStruct((D_MODEL, D_MODEL), BF16),
                   jax.ShapeDtypeStruct((D_MODEL, D_MODEL), BF16)),
        grid=(steps,),
        in_specs=in_specs,
        out_specs=(pl.BlockSpec((ts, D_MODEL), lambda t: (t, 0)), weight_slice, weight_slice),
        scratch_shapes=[
            pltpu.VMEM((GLA_HEADS, GLA_DK, GLA_DV), F32),
            pltpu.VMEM((CONV_GROUPS, CONV_HIST + ts, CONV_GROUP_WIDTH), F32),
        ],
        compiler_params=pltpu.CompilerParams(
            dimension_semantics=("arbitrary",), vmem_limit_bytes=VMEM_LIMIT),
        name="gla_conv_mixer",
    )(*args, w_out, w_gate)


def _output_call(h2d, mix2d, p2d, w_out, ple_norm, w_gate, b_gate, w_ple, final_norm, apply_final_norm):
    tokens = h2d.shape[0]
    tm = TOKEN_TILE
    row = lambda a: a.reshape(1, -1)
    tile = lambda width: pl.BlockSpec((tm, width), lambda i: (i, 0))
    weights = (w_out, row(ple_norm), w_gate, row(b_gate), w_ple, row(final_norm))
    return pl.pallas_call(
        functools.partial(_output_kernel, apply_final_norm=apply_final_norm),
        out_shape=jax.ShapeDtypeStruct((tokens, D_MODEL), F32),
        grid=(tokens // tm,),
        in_specs=[tile(D_MODEL), tile(D_MODEL), tile(PLE_DIM)] + [_resident(a.shape) for a in weights],
        out_specs=tile(D_MODEL),
        compiler_params=pltpu.CompilerParams(
            dimension_semantics=("arbitrary",), vmem_limit_bytes=VMEM_LIMIT),
        name="outproj_ple",
    )(h2d, mix2d, p2d, *weights)


def kernel(x, p, norm_mix, w_in, w_alpha, b_alpha, gla_norm, conv_w, conv_b, conv_ln_g, conv_ln_b,
           w_out, ple_norm, w_ple_gate, b_ple_gate, w_ple, final_norm):
    bsz, seq, _ = x.shape
    depth = p.shape[0]
    tokens = bsz * seq
    assert seq % SEQ_TILE == 0 and tokens % TOKEN_TILE == 0
    tril = jnp.asarray(np.tril(np.ones((SEQ_TILE, SEQ_TILE), np.float32)), BF16)
    lvl = jnp.asarray(_level_map(SEQ_TILE), BF16)

    h = x.reshape(tokens, D_MODEL)
    for i in range(depth):
        w_alpha_p = jnp.concatenate(
            [w_alpha[i], jnp.zeros((LANES - GATE_RANK, GLA_KEY_WIDTH), w_alpha.dtype)], axis=0).astype(BF16)
        conv_w_p = jnp.concatenate([conv_w[i], jnp.zeros((1, CONV_WIDTH), conv_w.dtype)], axis=0)
        w_in_t = jnp.swapaxes(w_in[i], 0, 1)
        mix, w_out16, w_gate16 = _mixer_call(
            h, seq, norm_mix[i], _win_relayout_call(w_in_t), w_alpha_p, b_alpha[i], gla_norm[i], conv_w_p,
            conv_b[i], conv_ln_g[i], conv_ln_b[i], tril, lvl, w_out[i], w_ple_gate[i])
        h = _output_call(
            h, mix, p[i].reshape(tokens, PLE_DIM), w_out16, ple_norm[i], w_gate16, b_ple_gate[i],
            w_ple[i].astype(BF16), final_norm, apply_final_norm=(i == depth - 1))
    return h.reshape(bsz, seq, D_MODEL)
```

```python
import functools
import math

import numpy as np
import jax
import jax.numpy as jnp
from jax import lax
from jax.experimental import pallas as pl
from jax.experimental.pallas import tpu as pltpu

F32 = jnp.float32
BF16 = jnp.bfloat16

D_MODEL = 2048
PLE_DIM = 256
GLA_WIDTH = 1024
CONV_WIDTH = 1024
GLA_HEADS = 4
GLA_DV = GLA_WIDTH // GLA_HEADS
GLA_DK = GLA_DV // 2
GLA_KEY_WIDTH = GLA_HEADS * GLA_DK
GATE_RANK = 16
GATE_TAU = 16.0
CONV_K = 31
CONV_GROUPS = 8
CONV_GROUP_WIDTH = CONV_WIDTH // CONV_GROUPS
EPS = 1e-6
LOG2E = math.log2(math.e)

SUBLANES = 8
LANES = 128

Q_END = GLA_KEY_WIDTH
K_END = Q_END + GLA_KEY_WIDTH
V_END = K_END + GLA_WIDTH
ALOW_END = V_END + GATE_RANK
GA_END = ALOW_END + GLA_WIDTH
CVAL_END = GA_END + CONV_WIDTH
CGATE_END = CVAL_END + CONV_WIDTH
D_IN = CGATE_END + CONV_WIDTH

RELAYOUT_COLS = 256
ROW_ALIGN = 16
CONV_PARTS = 4
CONV_PART = CONV_WIDTH // CONV_PARTS
assert CONV_PART % RELAYOUT_COLS == 0 and CONV_PART % CONV_GROUP_WIDTH == 0

O_Q = 0
O_K = O_Q + GLA_KEY_WIDTH
O_V = O_K + GLA_KEY_WIDTH
O_GLU = O_V + GLA_WIDTH
O_GA = O_GLU + 2 * CONV_WIDTH
O_GB = O_GA + GLA_WIDTH
O_ALOW = O_GB + CONV_WIDTH
D_IN_PAD = O_ALOW + RELAYOUT_COLS

SEQ_TILE = 256
TOKEN_TILE = 512
X_AHEAD = 2
X_SLOTS = X_AHEAD + 1
CONV_HIST = 32
DIAG_LEVEL = 0
VMEM_LIMIT = 56 * 1024 * 1024


def _sigmoid(x):
    return 1.0 / (1.0 + jnp.exp(-x))


def _dot(a, b):
    return jnp.dot(a, b, preferred_element_type=F32)


def _dot_nt(a, b):
    return lax.dot_general(a, b, (((1,), (1,)), ((), ())), preferred_element_type=F32)


def _dot_tn(a, b):
    return lax.dot_general(a, b, (((0,), (0,)), ((), ())), preferred_element_type=F32)


def _level_sizes(ts):
    sizes = []
    s = ts // 2
    while s >= SUBLANES:
        sizes.append(s)
        s //= 2
    return tuple(sizes)


def _level_map(ts):
    i = np.arange(ts)[:, None]
    j = np.arange(ts)[None, :]
    x = np.bitwise_xor(i, j)
    top = np.floor(np.log2(np.maximum(x, 1))).astype(np.int32)
    lvl = np.where(j > i, -1, np.where(i // SUBLANES == j // SUBLANES, DIAG_LEVEL, top))
    return lvl.astype(np.int32)


def _mixer_kernel(x_ref, nm_ref, win_ref, wal_ref, bal_ref, gn_ref, cw_ref, cb_ref, lg_ref, lb_ref,
                  tril_ref, lvl_ref, wout_ref, wgate_ref, mix_ref, wout16_ref, wgate16_ref, state_ref,
                  cbuf_ref, xbuf_ref, xsem_ref, *, tiles_per_seq, n_tiles):
    ts = SEQ_TILE
    nblk = ts // SUBLANES
    wout16_ref[...] = wout_ref[...].astype(BF16)
    wgate16_ref[...] = wgate_ref[...].astype(BF16)

    @pl.when(lax.rem(pl.program_id(0), tiles_per_seq) == 0)
    def _():
        state_ref[...] = jnp.zeros_like(state_ref)
        cbuf_ref[:, 0:CONV_HIST, :] = jnp.zeros((CONV_GROUPS, CONV_HIST, CONV_GROUP_WIDTH), F32)

    t = pl.program_id(0)

    def x_copy(tile):
        slot = lax.rem(tile, X_SLOTS)
        return pltpu.make_async_copy(
            x_ref.at[pl.ds(pl.multiple_of(tile * ts, ts), ts), :], xbuf_ref.at[slot], xsem_ref.at[slot])

    @pl.when(t == 0)
    def _():
        for first in range(X_AHEAD):
            x_copy(jnp.int32(first)).start()

    @pl.when(t + X_AHEAD < n_tiles)
    def _():
        x_copy(t + X_AHEAD).start()

    x_copy(t).wait()
    x = xbuf_ref[lax.rem(t, X_SLOTS)]
    ms = jnp.mean(x * x, axis=-1, keepdims=True)
    u = (x * lax.rsqrt(ms + EPS) * nm_ref[...]).astype(BF16)

    def proj(lo, width):
        return _dot(u, win_ref[:, lo:lo + width])

    acc_groups = []
    for part in range(CONV_PARTS):
        glu = proj(O_GLU + 2 * part * CONV_PART, 2 * CONV_PART)
        c = glu[:, :CONV_PART] * _sigmoid(glu[:, CONV_PART:])
        for sub in range(CONV_PART // CONV_GROUP_WIDTH):
            gi = part * (CONV_PART // CONV_GROUP_WIDTH) + sub
            gsl = slice(gi * CONV_GROUP_WIDTH, (gi + 1) * CONV_GROUP_WIDTH)
            cbuf_ref[gi, CONV_HIST:CONV_HIST + ts, :] = c[:, sub * CONV_GROUP_WIDTH:(sub + 1) * CONV_GROUP_WIDTH]
            acc = jnp.broadcast_to(cb_ref[:, gsl], (ts, CONV_GROUP_WIDTH))
            for tap in range(CONV_K):
                start = CONV_HIST - (CONV_K - 1) + tap
                acc = acc + cw_ref[tap:tap + 1, gsl] * cbuf_ref[gi, start:start + ts, :]
            cbuf_ref[gi, 0:CONV_HIST, :] = cbuf_ref[gi, ts:ts + CONV_HIST, :]
            acc_groups.append(acc)

    a_low = proj(O_ALOW, LANES).astype(BF16)
    z = _dot(a_low, wal_ref[...]) + bal_ref[...]
    g = (jnp.minimum(z, 0.0) - jnp.log1p(jnp.exp(-jnp.abs(z)))) * (1.0 / GATE_TAU)
    g_hi = g.astype(BF16)
    r1 = g - g_hi.astype(F32)
    g_mid = r1.astype(BF16)
    g_lo = (r1 - g_mid.astype(F32)).astype(BF16)
    tril3 = jnp.concatenate([tril_ref[...]] * 3, axis=1)
    b_all = _dot(tril3, jnp.concatenate([g_hi, g_mid, g_lo], axis=0)) * LOG2E

    qkv = proj(O_Q, O_GLU - O_Q)
    q_all = qkv[:, O_Q:O_K] * (GLA_DK ** -0.5)
    k_all = qkv[:, O_K:O_V]
    v_all = qkv[:, O_V:O_GLU].astype(BF16)
    ga_all = proj(O_GA, GLA_WIDTH)
    lvl = lvl_ref[...]
    lane_mod = lax.broadcasted_iota(jnp.int32, (nblk, SUBLANES, LANES), 2) & (SUBLANES - 1)

    for h in range(GLA_HEADS):
        ksl = slice(h * GLA_DK, (h + 1) * GLA_DK)
        vsl = slice(h * GLA_DV, (h + 1) * GLA_DV)
        q = q_all[:, ksl]
        k = k_all[:, ksl]
        b = b_all[:, ksl]
        v = v_all[:, vsl]

        p = jnp.zeros((ts, ts), BF16)
        for s in _level_sizes(ts):
            b3 = b.reshape(ts // (2 * s), 2 * s, GLA_DK)
            a = jnp.exp2(-jnp.abs(b3 - b3[:, s - 1:s, :])).reshape(ts, GLA_DK)
            p_l = _dot_nt((q * a).astype(BF16), (k * a).astype(BF16))
            p = jnp.where(lvl == float(np.log2(s)), p_l.astype(BF16), p)

        b3 = b.reshape(nblk, SUBLANES, GLA_DK)
        q3 = q.reshape(nblk, SUBLANES, GLA_DK)
        k3 = k.reshape(nblk, SUBLANES, GLA_DK)
        compact = jnp.zeros((nblk, SUBLANES, LANES), F32)
        for d in range(SUBLANES):
            t = q3 * jnp.exp2(b3 - b3[:, d:d + 1, :]) * k3[:, d:d + 1, :]
            compact = jnp.where(lane_mod == d, jnp.sum(t, axis=-1, keepdims=True), compact)
        compact = compact.reshape(ts, LANES)
        p = jnp.where(lvl == float(DIAG_LEVEL),
                      jnp.concatenate([compact.astype(BF16)] * (ts // LANES), axis=1), p)

        st = state_ref[h]
        b_last = b[ts - 1:ts, :]
        o = _dot(p, v) + _dot((q * jnp.exp2(b)).astype(BF16), st.astype(BF16))
        k_hat = (k * jnp.exp2(b_last - b)).astype(BF16)
        decay = jnp.transpose(jnp.broadcast_to(jnp.exp2(b_last), (SUBLANES, GLA_DK)))[:, 0:1]
        state_ref[h] = st * decay + _dot_tn(k_hat, v)

        o_ms = jnp.mean(o * o, axis=-1, keepdims=True)
        o_n = o * lax.rsqrt(o_ms + EPS) * gn_ref[...]
        ga = ga_all[:, vsl]
        mix_ref[:, vsl] = (o_n * (ga * _sigmoid(ga))).astype(BF16)

    gb_all = proj(O_GB, CONV_WIDTH)
    for gi in range(CONV_GROUPS):
        gsl = slice(gi * CONV_GROUP_WIDTH, (gi + 1) * CONV_GROUP_WIDTH)
        cg = acc_groups[gi]
        mu = jnp.mean(cg, axis=-1, keepdims=True)
        dlt = cg - mu
        var = jnp.mean(dlt * dlt, axis=-1, keepdims=True)
        y = dlt * lax.rsqrt(var + EPS) * lg_ref[:, gsl] + lb_ref[:, gsl]
        gb = gb_all[:, gsl]
        mix_ref[:, GLA_WIDTH + gi * CONV_GROUP_WIDTH:GLA_WIDTH + (gi + 1) * CONV_GROUP_WIDTH] = (
            (y * _sigmoid(y)) * (gb * _sigmoid(gb))).astype(BF16)


def _output_kernel(h_ref, mix_ref, p_ref, wout_ref, pn_ref, wg_ref, bg_ref, wp_ref, fn_ref, o_ref, *,
                   apply_final_norm):
    h = h_ref[...] + _dot(mix_ref[...], wout_ref[...])
    ms = jnp.mean(h * h, axis=-1, keepdims=True)
    n = (h * lax.rsqrt(ms + EPS) * pn_ref[...]).astype(BF16)
    gate = _sigmoid(_dot(n, wg_ref[...]) + bg_ref[...])
    h = h + gate * _dot(p_ref[...].astype(BF16), wp_ref[...])
    if apply_final_norm:
        ms2 = jnp.mean(h * h, axis=-1, keepdims=True)
        h = h * lax.rsqrt(ms2 + EPS) * fn_ref[...]
    o_ref[...] = h


def _win_relayout_kernel(src_rows_ref, wt_ref, o_ref):
    del src_rows_ref
    o_ref[...] = wt_ref[...].T.astype(BF16)


def _resident(shape):
    nd = len(shape)
    return pl.BlockSpec(shape, lambda *_: (0,) * nd, pipeline_mode=pl.Buffered(1))


def _win_relayout_call(w_in_t):
    segments = [(O_Q, 0, V_END), (O_GA, ALOW_END, GLA_WIDTH), (O_GB, CGATE_END, CONV_WIDTH),
                (O_ALOW, V_END, RELAYOUT_COLS)]
    for part in range(CONV_PARTS):
        segments.append((O_GLU + 2 * part * CONV_PART, GA_END + part * CONV_PART, CONV_PART))
        segments.append((O_GLU + (2 * part + 1) * CONV_PART, CVAL_END + part * CONV_PART, CONV_PART))
    src_rows = np.zeros((D_IN_PAD // RELAYOUT_COLS,), np.int32)
    for dst, src, width in segments:
        for off in range(0, width, RELAYOUT_COLS):
            assert (src + off) % ROW_ALIGN == 0 and src + off + RELAYOUT_COLS <= D_IN
            src_rows[(dst + off) // RELAYOUT_COLS] = (src + off) // ROW_ALIGN
    return pl.pallas_call(
        _win_relayout_kernel,
        out_shape=jax.ShapeDtypeStruct((D_MODEL, D_IN_PAD), BF16),
        grid_spec=pltpu.PrefetchScalarGridSpec(
            num_scalar_prefetch=1,
            grid=(D_IN_PAD // RELAYOUT_COLS,),
            in_specs=[pl.BlockSpec((pl.Element(RELAYOUT_COLS), pl.Element(D_MODEL)),
                                   lambda j, rows: (rows[j] * ROW_ALIGN, 0))],
            out_specs=pl.BlockSpec((D_MODEL, RELAYOUT_COLS), lambda j, rows: (0, j)),
        ),
        compiler_params=pltpu.CompilerParams(dimension_semantics=("arbitrary",)),
        name="win_relayout",
    )(jnp.asarray(src_rows), w_in_t)


def _mixer_call(h2d, seq, norm_mix, w_in_r, w_alpha_p, b_alpha, gla_norm, conv_w, conv_b, ln_g, ln_b,
                tril, lvl, w_out, w_gate):
    tokens = h2d.shape[0]
    ts = SEQ_TILE
    steps = tokens // ts
    cast_rows = D_MODEL // steps
    assert cast_rows * steps == D_MODEL and cast_rows % (2 * SUBLANES) == 0
    row = lambda a: a.reshape(1, -1)
    args = (h2d, row(norm_mix), w_in_r, w_alpha_p, row(b_alpha), row(gla_norm), conv_w, row(conv_b),
            row(ln_g), row(ln_b), tril, lvl)
    assert steps >= X_AHEAD
    in_specs = [pl.BlockSpec(memory_space=pl.ANY)]
    in_specs += [_resident(a.shape) for a in args[1:]]
    weight_slice = pl.BlockSpec((cast_rows, D_MODEL), lambda t: (t, 0))
    in_specs += [weight_slice, weight_slice]
    return pl.pallas_call(
        functools.partial(_mixer_kernel, tiles_per_seq=seq // ts, n_tiles=steps),
        out_shape=(jax.ShapeDtypeStruct((tokens, D_MODEL), BF16),
                   jax.ShapeDtypeStruct((D_MODEL, D_MODEL), BF16),
                   jax.ShapeDtypeStruct((D_MODEL, D_MODEL), BF16)),
        grid=(steps,),
        in_specs=in_specs,
        out_specs=(pl.BlockSpec((ts, D_MODEL), lambda t: (t, 0)), weight_slice, weight_slice),
        scratch_shapes=[
            pltpu.VMEM((GLA_HEADS, GLA_DK, GLA_DV), F32),
            pltpu.VMEM((CONV_GROUPS, CONV_HIST + ts, CONV_GROUP_WIDTH), F32),
            pltpu.VMEM((X_SLOTS, ts, D_MODEL), F32),
            pltpu.SemaphoreType.DMA((X_SLOTS,)),
        ],
        compiler_params=pltpu.CompilerParams(
            dimension_semantics=("arbitrary",), vmem_limit_bytes=VMEM_LIMIT),
        name="gla_conv_mixer",
    )(*args, w_out, w_gate)


def _output_call(h2d, mix2d, p2d, w_out, ple_norm, w_gate, b_gate, w_ple, final_norm, apply_final_norm):
    tokens = h2d.shape[0]
    tm = TOKEN_TILE
    row = lambda a: a.reshape(1, -1)
    tile = lambda width: pl.BlockSpec((tm, width), lambda i: (i, 0))
    weights = (w_out, row(ple_norm), w_gate, row(b_gate), w_ple, row(final_norm))
    return pl.pallas_call(
        functools.partial(_output_kernel, apply_final_norm=apply_final_norm),
        out_shape=jax.ShapeDtypeStruct((tokens, D_MODEL), F32),
        grid=(tokens // tm,),
        in_specs=[tile(D_MODEL), tile(D_MODEL), tile(PLE_DIM)] + [_resident(a.shape) for a in weights],
        out_specs=tile(D_MODEL),
        compiler_params=pltpu.CompilerParams(
            dimension_semantics=("arbitrary",), vmem_limit_bytes=VMEM_LIMIT),
        name="outproj_ple",
    )(h2d, mix2d, p2d, *weights)


def kernel(x, p, norm_mix, w_in, w_alpha, b_alpha, gla_norm, conv_w, conv_b, conv_ln_g, conv_ln_b,
           w_out, ple_norm, w_ple_gate, b_ple_gate, w_ple, final_norm):
    bsz, seq, _ = x.shape
    depth = p.shape[0]
    tokens = bsz * seq
    assert seq % SEQ_TILE == 0 and tokens % TOKEN_TILE == 0
    tril = jnp.asarray(np.tril(np.ones((SEQ_TILE, SEQ_TILE), np.float32)), BF16)
    lvl = jnp.asarray(_level_map(SEQ_TILE), BF16)

    h = x.reshape(tokens, D_MODEL)
    for i in range(depth):
        w_alpha_p = jnp.concatenate(
            [w_alpha[i], jnp.zeros((LANES - GATE_RANK, GLA_KEY_WIDTH), w_alpha.dtype)], axis=0).astype(BF16)
        conv_w_p = jnp.concatenate([conv_w[i], jnp.zeros((1, CONV_WIDTH), conv_w.dtype)], axis=0)
        w_in_t = jnp.swapaxes(w_in[i], 0, 1)
        mix, w_out16, w_gate16 = _mixer_call(
            h, seq, norm_mix[i], _win_relayout_call(w_in_t), w_alpha_p, b_alpha[i], gla_norm[i], conv_w_p,
            conv_b[i], conv_ln_g[i], conv_ln_b[i], tril, lvl, w_out[i], w_ple_gate[i])
        h = _output_call(
            h, mix, p[i].reshape(tokens, PLE_DIM), w_out16, ple_norm[i], w_gate16, b_ple_gate[i],
            w_ple[i].astype(BF16), final_norm, apply_final_norm=(i == depth - 1))
    return h.reshape(bsz, seq, D_MODEL)
```
